```python
import jax, jax.numpy as jnp
from jax import lax
import numpy as np

D_MODEL = 1024
BATCH = 8
SEQ = 4096
DEPTH = 2

CHUNK = 64
P_DIM = 256
EPS = 1e-6
HG_HEADS = 6
HG_DK = 128
HG_DV = 128
D_HG = HG_HEADS * HG_DV
POOL_WINDOWS = (2, 4, 8, 16)
POOL_GROUPS = 4
POOL_CH = 128
D_POOL = POOL_GROUPS * POOL_CH
SSD_HEADS = 12
SSD_HEAD_DIM = 64
D_SSD = SSD_HEADS * SSD_HEAD_DIM
SSD_STATE = 128
SSD_GROUPS = 4
SSD_HPG = SSD_HEADS // SSD_GROUPS
SSD_CONV = 4
SSD_CONV_DIM = D_SSD + 2 * SSD_GROUPS * SSD_STATE
D_MIX = D_HG + D_POOL + D_SSD
IN_SPLITS = (HG_HEADS * HG_DK, HG_HEADS * HG_DK, D_HG, D_HG, D_POOL, D_POOL, SSD_CONV_DIM, SSD_HEADS, D_SSD)
N_IN = 6668

kernel_name = "hymba_style_hgrn2_pool_ssd_trunk"


def rms_norm(x, w):
    x32 = x.astype(jnp.float32)
    y = x32 * lax.rsqrt(jnp.mean(x32 * x32, axis=-1, keepdims=True) + EPS)
    return (y * w.astype(jnp.float32)).astype(x.dtype)


def group_rms_norm(y, w, groups):
    b, s, d = y.shape
    y32 = y.astype(jnp.float32).reshape(b, s, groups, d // groups)
    y32 = y32 * lax.rsqrt(jnp.mean(y32 * y32, axis=-1, keepdims=True) + EPS)
    return y32.reshape(b, s, d) * w.astype(jnp.float32)


def hgrn2_mixer(q, f_logit, i_in, lb):
    b, s, _ = q.shape
    nc = s // CHUNK
    lb = lb.astype(jnp.float32)
    log_f = jnp.logaddexp(jnp.log(lb), jnp.log1p(-lb) + jax.nn.log_sigmoid(f_logit.astype(jnp.float32)))
    k = -jnp.expm1(log_f)

    def to_chunks(t, d):
        return t.astype(jnp.float32).reshape(b, nc, CHUNK, HG_HEADS, d).transpose(1, 0, 3, 2, 4)

    qc, kc, vc, gc = to_chunks(q, HG_DK), to_chunks(k, HG_DK), to_chunks(i_in, HG_DV), to_chunks(log_f, HG_DK)
    causal = jnp.tril(jnp.ones((CHUNK, CHUNK), dtype=bool))

    def step(state, inp):
        qt, kt, vt, gt = inp
        cum = jnp.cumsum(gt, axis=2)
        inter = jnp.einsum('bhtk,bhkv->bhtv', qt * jnp.exp(cum), state)
        rel = cum[:, :, :, None, :] - cum[:, :, None, :, :]
        decay = jnp.exp(jnp.where(causal[:, :, None], rel, -jnp.inf))
        scores = jnp.einsum('bhtk,bhsk,bhtsk->bhts', qt, kt, decay)
        intra = jnp.einsum('bhts,bhsv->bhtv', scores, vt)
        last = cum[:, :, -1:, :]
        state = jnp.exp(last[:, :, 0, :])[..., None] * state + jnp.einsum(
            'bhsk,bhsv->bhkv', kt * jnp.exp(last - cum), vt)
        return state, inter + intra

    s0 = jnp.zeros((b, HG_HEADS, HG_DK, HG_DV), jnp.float32)
    _, o = lax.scan(step, s0, (qc, kc, vc, gc))
    return o.transpose(1, 0, 3, 2, 4).reshape(b, s, D_HG)


def pool_mixer(u, pool_w, pool_scale):
    b, s, _ = u.shape
    u32 = u.astype(jnp.float32).reshape(b, s, POOL_GROUPS, POOL_CH)
    cs = jnp.cumsum(u32, axis=1)
    pos = jnp.arange(1, s + 1, dtype=jnp.float32)
    outs = []
    for g, win in enumerate(POOL_WINDOWS):
        c = cs[:, :, g, :]
        lagged = jnp.pad(c, ((0, 0), (win, 0), (0, 0)))[:, :s]
        count = jnp.minimum(pos, float(win))[None, :, None]
        outs.append((c - lagged) / count - u32[:, :, g, :])
    pooled = jnp.stack(outs, axis=2)
    y = jnp.einsum('bsgc,gcd->bsgd', pooled, pool_w.astype(jnp.float32)) * pool_scale.astype(jnp.float32)
    return y.reshape(b, s, D_POOL)


def ssd_mixer(xbc, dt_raw, conv_w, conv_b, dt_bias, a_log, d_skip):
    b, s, _ = xbc.shape
    nc = s // CHUNK
    xbc = lax.conv_general_dilated(
        xbc, conv_w[:, None, :], window_strides=(1,), padding=[(SSD_CONV - 1, 0)],
        dimension_numbers=('NWC', 'WIO', 'NWC'), feature_group_count=SSD_CONV_DIM) + conv_b
    xbc = jax.nn.silu(xbc.astype(jnp.float32))
    xs, bm, cm = jnp.split(xbc, [D_SSD, D_SSD + SSD_GROUPS * SSD_STATE], axis=-1)
    xs = xs.reshape(b, nc, CHUNK, SSD_GROUPS, SSD_HPG, SSD_HEAD_DIM)
    bm = bm.reshape(b, nc, CHUNK, SSD_GROUPS, SSD_STATE)
    cm = cm.reshape(b, nc, CHUNK, SSD_GROUPS, SSD_STATE)
    dt = jax.nn.softplus(dt_raw.astype(jnp.float32) + dt_bias.astype(jnp.float32))
    a = -jnp.exp(a_log.astype(jnp.float32))
    dt_c = dt.reshape(b, nc, CHUNK, SSD_GROUPS, SSD_HPG)
    dta = (dt_c * a.reshape(SSD_GROUPS, SSD_HPG)).transpose(0, 3, 4, 1, 2)
    xdt = xs * dt_c[..., None]
    cum = jnp.cumsum(dta, axis=-1)
    causal = jnp.tril(jnp.ones((CHUNK, CHUNK), dtype=bool))
    seg = jnp.exp(jnp.where(causal, cum[..., :, None] - cum[..., None, :], -jnp.inf))
    cb = jnp.einsum('bctgn,bcsgn->bgcts', cm, bm)
    y_diag = jnp.einsum('bgcts,bgrcts,bcsgrp->bctgrp', cb, seg, xdt)
    last = cum[..., -1:]
    states = jnp.einsum('bcsgn,bgrcs,bcsgrp->bcgrpn', bm, jnp.exp(last - cum), xdt)
    chunk_decay = jnp.exp(last[..., 0])

    def carry(h, inp):
        dec, st = inp
        return dec[..., None, None] * h + st, h

    h0 = jnp.zeros((b, SSD_GROUPS, SSD_HPG, SSD_HEAD_DIM, SSD_STATE), jnp.float32)
    _, h_prev = lax.scan(carry, h0, (chunk_decay.transpose(3, 0, 1, 2), states.transpose(1, 0, 2, 3, 4, 5)))
    y_off = jnp.einsum('bctgn,bgrct,cbgrpn->bctgrp', cm, jnp.exp(cum), h_prev)
    y = y_diag + y_off + xs * d_skip.astype(jnp.float32).reshape(SSD_GROUPS, SSD_HPG)[:, :, None]
    return y.reshape(b, s, D_SSD)


def hybrid_layer(h, p_i, lb, norm_w, w_in, hg_norm_w, pool_w, pool_scale, conv_w, conv_b,
                 dt_bias, a_log, d_skip, ssd_norm_w, w_out, w_pe, w_pg):
    u = rms_norm(h, norm_w)
    proj = u @ w_in
    q, f_logit, i_in, g_hg, u_pool, g_pool, xbc, dt_raw, z = jnp.split(
        proj, [int(v) for v in np.cumsum(IN_SPLITS)[:-1]], axis=-1)
    o_hg = group_rms_norm(hgrn2_mixer(q, f_logit, i_in, lb), hg_norm_w, HG_HEADS) * jax.nn.silu(g_hg.astype(jnp.float32))
    o_pool = pool_mixer(u_pool, pool_w, pool_scale) * jax.nn.silu(g_pool.astype(jnp.float32))
    y_ssd = ssd_mixer(xbc, dt_raw, conv_w, conv_b, dt_bias, a_log, d_skip) * jax.nn.silu(z.astype(jnp.float32))
    o_ssd = group_rms_norm(y_ssd, ssd_norm_w, SSD_GROUPS)
    mixed = jnp.concatenate([o_hg, o_pool, o_ssd], axis=-1).astype(h.dtype)
    h = h + mixed @ w_out
    gate = jax.nn.sigmoid(h @ w_pg)
    return h + gate * (p_i @ w_pe)


def setup_inputs(seed: int = 0) -> dict:
    key = jax.random.key(seed)
    ks = jax.random.split(key, 20)
    f32 = jnp.float32
    nrm = lambda k, shape, scale: jax.random.normal(k, shape, f32) * scale
    dt0 = jnp.exp(jax.random.uniform(ks[9], (DEPTH, SSD_HEADS), f32, np.log(1e-3), np.log(1e-1)))
    return {
        'x': nrm(ks[0], (BATCH, SEQ, D_MODEL), 1.0),
        'p': nrm(ks[1], (DEPTH, BATCH, SEQ, P_DIM), 1.0),
        'norm_w': 1.0 + nrm(ks[2], (DEPTH, D_MODEL), 0.02),
        'w_in': nrm(ks[3], (DEPTH, D_MODEL, N_IN), D_MODEL ** -0.5),
        'hg_lb': 1.0 + nrm(ks[4], (DEPTH, HG_HEADS * HG_DK), 0.5),
        'hg_norm_w': 1.0 + nrm(ks[5], (DEPTH, D_HG), 0.02),
        'pool_w': nrm(ks[6], (DEPTH, POOL_GROUPS, POOL_CH, POOL_CH), POOL_CH ** -0.5),
        'pool_scale': 1.0 + nrm(ks[7], (DEPTH, POOL_GROUPS, POOL_CH), 0.1),
        'conv_w': nrm(ks[8], (DEPTH, SSD_CONV, SSD_CONV_DIM), SSD_CONV ** -0.5),
        'conv_b': nrm(ks[10], (DEPTH, SSD_CONV_DIM), 0.01),
        'dt_bias': dt0 + jnp.log(-jnp.expm1(-dt0)),
        'a_log': jnp.log(jax.random.uniform(ks[11], (DEPTH, SSD_HEADS), f32, 1.0, 16.0)),
        'd_skip': 1.0 + nrm(ks[12], (DEPTH, SSD_HEADS), 0.1),
        'ssd_norm_w': 1.0 + nrm(ks[13], (DEPTH, D_SSD), 0.02),
        'w_out': nrm(ks[14], (DEPTH, D_MIX, D_MODEL), D_MIX ** -0.5),
        'w_pe': nrm(ks[15], (DEPTH, P_DIM, D_MODEL), P_DIM ** -0.5),
        'w_pg': nrm(ks[16], (DEPTH, D_MODEL, D_MODEL), D_MODEL ** -0.5),
        'final_norm_w': 1.0 + nrm(ks[17], (D_MODEL,), 0.02),
    }


def reference(x, p, norm_w, w_in, hg_lb, hg_norm_w, pool_w, pool_scale, conv_w, conv_b,
              dt_bias, a_log, d_skip, ssd_norm_w, w_out, w_pe, w_pg, final_norm_w):
    lb_all = jnp.cumsum(jax.nn.softmax(hg_lb.astype(jnp.float32), axis=0), axis=0)
    lb_all = lb_all - lb_all[0]
    h = x
    for i in range(DEPTH):
        h = hybrid_layer(h, p[i], lb_all[i], norm_w[i], w_in[i], hg_norm_w[i], pool_w[i], pool_scale[i],
                         conv_w[i], conv_b[i], dt_bias[i], a_log[i], d_skip[i], ssd_norm_w[i],
                         w_out[i], w_pe[i], w_pg[i])
    return rms_norm(h, final_norm_w)
```

```python
import functools

import jax
import jax.numpy as jnp
from jax import lax
from jax.experimental import pallas as pl
from jax.experimental.pallas import tpu as pltpu

F32 = jnp.float32
BF16 = jnp.bfloat16

CHUNK = 64
EPS = 1e-6
D_MODEL = 1024
P_DIM = 256
HG_HEADS = 6
HG_D = 128
D_HG = HG_HEADS * HG_D
POOL_WINDOWS = (2, 4, 8, 16)
POOL_CH = 128
D_POOL = len(POOL_WINDOWS) * POOL_CH
POOL_HALO = 16
SSD_HEADS = 12
SSD_HEAD_DIM = 64
SSD_GROUPS = 4
SSD_HPG = SSD_HEADS // SSD_GROUPS
SSD_GW = SSD_HPG * SSD_HEAD_DIM
SSD_GWP = 256
D_SSD = SSD_HEADS * SSD_HEAD_DIM
D_SSD_P = SSD_GROUPS * SSD_GWP
SSD_STATE = 128
SSD_CONV = 4
CONV_HALO = 8
LANES = 128
SSD_XBC_P = D_SSD_P + 2 * SSD_GROUPS * SSD_STATE
SSD_COLS = SSD_XBC_P + D_SSD_P + LANES
VMEM_LIMIT = 48 * 1024 * 1024


def _dot(a, b):
    return jnp.dot(a.astype(BF16), b.astype(BF16), preferred_element_type=F32)


def _dot_nt(a, b):
    return lax.dot_general(a.astype(BF16), b.astype(BF16), (((1,), (1,)), ((), ())),
                           preferred_element_type=F32)


def _dot_tn(a, b):
    return lax.dot_general(a.astype(BF16), b.astype(BF16), (((0,), (0,)), ((), ())),
                           preferred_element_type=F32)


def _split3(x):
    hi = x.astype(BF16)
    r1 = x - hi.astype(F32)
    mid = r1.astype(BF16)
    lo = (r1 - mid.astype(F32)).astype(BF16)
    return hi, mid, lo


def _sel_left(m, x):
    hi, mid, lo = _split3(x)
    d = lambda p: jnp.dot(m, p, preferred_element_type=F32)
    return d(hi) + (d(mid) + d(lo))


def _sel_right(x, m):
    hi, mid, lo = _split3(x)
    d = lambda p: jnp.dot(p, m, preferred_element_type=F32)
    return d(hi) + (d(mid) + d(lo))


def _sigmoid(x):
    return 1.0 / (1.0 + jnp.exp(-x))


def _silu(x):
    return x * _sigmoid(x)


def _softplus(x):
    return jnp.maximum(x, 0.0) + jnp.log1p(jnp.exp(-jnp.abs(x)))


def _tril_bf16(n):
    r = lax.broadcasted_iota(jnp.int32, (n, n), 0)
    c = lax.broadcasted_iota(jnp.int32, (n, n), 1)
    return (r >= c).astype(BF16)


def _norm_kernel(x_ref, w_ref, o_ref):
    x = x_ref[...]
    y = x * lax.rsqrt(jnp.mean(x * x, axis=-1, keepdims=True) + EPS)
    o_ref[...] = (y * w_ref[...]).astype(o_ref.dtype)


def _rms_norm_call(x2d, w, tm):
    n, d = x2d.shape
    return pl.pallas_call(
        _norm_kernel,
        grid=(n // tm,),
        in_specs=[pl.BlockSpec((tm, d), lambda i: (i, 0)),
                  pl.BlockSpec((1, d), lambda i: (0, 0))],
        out_specs=pl.BlockSpec((tm, d), lambda i: (i, 0)),
        out_shape=jax.ShapeDtypeStruct((n, d), BF16),
        compiler_params=pltpu.CompilerParams(dimension_semantics=("parallel",)),
        name="rms_norm_in",
    )(x2d, w.reshape(1, d))


def _hgrn_kernel(u_ref, w_ref, loglb_ref, log1mlb_ref, nw_ref, o_ref, proj_ref, state_ref, *, n_chunks):
    @pl.when(pl.program_id(2) == 0)
    def _():
        state_ref[...] = jnp.zeros_like(state_ref)

    proj_ref[...] = jnp.dot(u_ref[0], w_ref[0], preferred_element_type=F32)

    row = lax.broadcasted_iota(jnp.int32, (CHUNK, HG_D), 0)
    r64 = lax.broadcasted_iota(jnp.int32, (CHUNK, CHUNK), 0)
    c64 = lax.broadcasted_iota(jnp.int32, (CHUNK, CHUNK), 1)
    tril = (r64 >= c64).astype(BF16)
    same32 = (r64 >> 5) == (c64 >> 5)
    same16 = (r64 >> 4) == (c64 >> 4)
    loglb = loglb_ref[0]
    log1mlb = log1mlb_ref[0]
    nw = nw_ref[0]

    def chunk(c, carry):
        r0 = pl.multiple_of(c * CHUNK, CHUNK)
        q = proj_ref[pl.ds(r0, CHUNK), 0:128]
        z = proj_ref[pl.ds(r0, CHUNK), 128:256]
        v = proj_ref[pl.ds(r0, CHUNK), 256:384]
        g = proj_ref[pl.ds(r0, CHUNK), 384:512]

        log_sig = jnp.minimum(z, 0.0) - jnp.log1p(jnp.exp(-jnp.abs(z)))
        b_ = log1mlb + log_sig
        log_f = jnp.maximum(loglb, b_) + jnp.log1p(jnp.exp(-jnp.abs(loglb - b_)))
        kk = 1.0 - jnp.exp(log_f)

        cum = _sel_left(tril, log_f)
        last = cum[CHUNK - 1:CHUNK, :]
        st = state_ref[...]
        inter = _dot_nt(q * jnp.exp(cum), st)
        state_ref[...] = st * jnp.exp(last) + _dot_tn(v, kk * jnp.exp(last - cum))

        def level(bs):
            half = bs // 2
            pieces = [jnp.broadcast_to(cum[b * bs + half - 1:b * bs + half, :], (bs, HG_D))
                      for b in range(CHUNK // bs)]
            ref = pieces[0] if len(pieces) == 1 else jnp.concatenate(pieces, axis=0)
            upper = (row & half) != 0
            d = cum - ref
            e = jnp.exp(jnp.where(upper, d, -d))
            qs = jnp.where(upper, q * e, 0.0)
            ks = jnp.where(upper, 0.0, kk * e)
            return _dot_nt(qs, ks)

        scores = level(64) + jnp.where(same32, level(32), 0.0) + jnp.where(same16, level(16), 0.0)
        for d in range(8):
            if d == 0:
                pr = q * kk
            else:
                valid = (row & 7) >= d
                kd = pltpu.roll(kk, d, 0)
                cd = pltpu.roll(cum, d, 0)
                pr = jnp.where(valid, q * kd * jnp.exp(jnp.where(valid, cum - cd, 0.0)), 0.0)
            rs = jnp.sum(pr, axis=1, keepdims=True)
            scores = scores + jnp.where(c64 == r64 - d, rs, 0.0)
        o = inter + _dot(scores, v)

        o = o * lax.rsqrt(jnp.mean(o * o, axis=-1, keepdims=True) + EPS) * nw
        o_ref[0, pl.ds(r0, CHUNK), :] = (o * _silu(g)).astype(o_ref.dtype)
        return carry

    lax.fori_loop(0, n_chunks, chunk, 0)


def _hgrn_call(u, w_hg, loglb, log1mlb, nw, t_blk):
    b, s, d = u.shape
    grid = (b, HG_HEADS, s // t_blk)
    vec = pl.BlockSpec((1, 1, HG_D), lambda bi, h, si: (h, 0, 0))
    return pl.pallas_call(
        functools.partial(_hgrn_kernel, n_chunks=t_blk // CHUNK),
        grid=grid,
        in_specs=[pl.BlockSpec((1, t_blk, d), lambda bi, h, si: (bi, si, 0)),
                  pl.BlockSpec((1, d, 4 * HG_D), lambda bi, h, si: (h, 0, 0)),
                  vec, vec, vec],
        out_specs=pl.BlockSpec((1, t_blk, HG_D), lambda bi, h, si: (bi, si, h)),
        out_shape=jax.ShapeDtypeStruct((b, s, D_HG), BF16),
        scratch_shapes=[pltpu.VMEM((t_blk, 4 * HG_D), F32),
                        pltpu.VMEM((HG_D, HG_D), F32)],
        compiler_params=pltpu.CompilerParams(
            dimension_semantics=("parallel", "parallel", "arbitrary"),
            vmem_limit_bytes=VMEM_LIMIT),
        name="hgrn2_mixer",
    )(u, w_hg, loglb, log1mlb, nw)


def _pool_kernel(u_ref, w_ref, pw_ref, ps_ref, o_ref, ext_ref, *, t_blk):
    si = pl.program_id(1)

    @pl.when(si == 0)
    def _():
        ext_ref[0:POOL_HALO, :] = jnp.zeros((POOL_HALO, D_POOL), F32)

    @pl.when(si > 0)
    def _():
        ext_ref[0:POOL_HALO, :] = ext_ref[t_blk:t_blk + POOL_HALO, :]

    proj = jnp.dot(u_ref[0], w_ref[...], preferred_element_type=F32)
    ext_ref[POOL_HALO:POOL_HALO + t_blk, :] = proj[:, :D_POOL]

    pos = (si * t_blk + 1 + lax.broadcasted_iota(jnp.int32, (t_blk, POOL_CH), 0)).astype(F32)
    for gi, win in enumerate(POOL_WINDOWS):
        lo = gi * POOL_CH
        acc = ext_ref[:, lo:lo + POOL_CH]
        shift = 1
        while shift < win:
            acc = acc + pltpu.roll(acc, shift, 0)
            shift *= 2
        cur = proj[:, lo:lo + POOL_CH]
        pooled = acc[POOL_HALO:, :] / jnp.minimum(pos, float(win)) - cur
        y = _dot(pooled, pw_ref[gi]) * ps_ref[:, lo:lo + POOL_CH]
        gate = proj[:, D_POOL + lo:D_POOL + lo + POOL_CH]
        o_ref[0, :, lo:lo + POOL_CH] = (y * _silu(gate)).astype(o_ref.dtype)


def _pool_call(u, w_pool, pool_w, pool_scale, t_blk):
    b, s, d = u.shape
    return pl.pallas_call(
        functools.partial(_pool_kernel, t_blk=t_blk),
        grid=(b, s // t_blk),
        in_specs=[pl.BlockSpec((1, t_blk, d), lambda bi, si: (bi, si, 0)),
                  pl.BlockSpec((d, 2 * D_POOL), lambda bi, si: (0, 0)),
                  pl.BlockSpec((len(POOL_WINDOWS), POOL_CH, POOL_CH), lambda bi, si: (0, 0, 0)),
                  pl.BlockSpec((1, D_POOL), lambda bi, si: (0, 0))],
        out_specs=pl.BlockSpec((1, t_blk, D_POOL), lambda bi, si: (bi, si, 0)),
        out_shape=jax.ShapeDtypeStruct((b, s, D_POOL), BF16),
        scratch_shapes=[pltpu.VMEM((t_blk + POOL_HALO, D_POOL), F32)],
        compiler_params=pltpu.CompilerParams(
            dimension_semantics=("parallel", "arbitrary"),
            vmem_limit_bytes=VMEM_LIMIT),
        name="pool_mixer",
    )(u, w_pool, pool_w, pool_scale)


def _ssd_kernel(u_ref, w_ref, cw_ref, cb_ref, dtb_ref, alogc_ref, aloge_ref, dskip_ref, nw_ref,
                exp_ref, exp128_ref, o_ref,
                ext_ref, act_ref, z_ref, xdt_ref, dtae_ref, dtac_ref, state_ref, *, t_blk):
    si = pl.program_id(1)

    @pl.when(si == 0)
    def _():
        ext_ref[0:CONV_HALO, :] = jnp.zeros((CONV_HALO, SSD_XBC_P), F32)
        state_ref[...] = jnp.zeros_like(state_ref)

    @pl.when(si > 0)
    def _():
        ext_ref[0:CONV_HALO, :] = ext_ref[t_blk:t_blk + CONV_HALO, :]

    proj = jnp.dot(u_ref[0], w_ref[...], preferred_element_type=F32)
    ext_ref[CONV_HALO:CONV_HALO + t_blk, :] = proj[:, :SSD_XBC_P]
    z_ref[...] = proj[:, SSD_XBC_P:SSD_XBC_P + D_SSD_P]

    e = ext_ref[...]
    xc = cw_ref[3:4, :] * e[CONV_HALO:, :] + cb_ref[...]
    for j in range(1, SSD_CONV):
        xc = xc + cw_ref[3 - j:4 - j, :] * pltpu.roll(e, j, 0)[CONV_HALO:, :]
    xc = _silu(xc)
    act_ref[...] = xc

    dt = _softplus(proj[:, SSD_XBC_P + D_SSD_P:] + dtb_ref[...])
    dt_e = _sel_right(dt, exp_ref[...])
    xdt_ref[...] = xc[:, :D_SSD_P] * dt_e
    dtae_ref[...] = dt_e * (-jnp.exp(aloge_ref[...]))
    dtac_ref[...] = dt * (-jnp.exp(alogc_ref[...]))

    r64 = lax.broadcasted_iota(jnp.int32, (CHUNK, CHUNK), 0)
    c64 = lax.broadcasted_iota(jnp.int32, (CHUNK, CHUNK), 1)
    causal = r64 >= c64
    tril = causal.astype(BF16)
    rp = lax.broadcasted_iota(jnp.int32, (LANES, CHUNK), 0)
    cp = lax.broadcasted_iota(jnp.int32, (LANES, CHUNK), 1)
    tril_pad = ((rp >= cp) & (rp < CHUNK)).astype(BF16)
    lane = lax.broadcasted_iota(jnp.int32, (CHUNK, SSD_GWP), 1)
    b_off = D_SSD_P
    c_off = D_SSD_P + SSD_GROUPS * SSD_STATE

    def chunk(c, carry):
        rows = pl.ds(pl.multiple_of(c * CHUNK, CHUNK), CHUNK)
        cum_e = _sel_left(tril, dtae_ref[rows, :])
        last_e = cum_e[CHUNK - 1:CHUNK, :]
        ecum = jnp.exp(cum_e)
        dec = jnp.exp(last_e - cum_e)
        elast = jnp.exp(last_e)
        cum_c = _sel_left(tril_pad, dtac_ref[rows, :])
        cum_t = cum_c.T
        cum_b = _sel_right(cum_c[:CHUNK, :], exp128_ref[...])

        for g in range(SSD_GROUPS):
            xl = g * SSD_GWP
            bmg = act_ref[rows, b_off + g * SSD_STATE:b_off + (g + 1) * SSD_STATE]
            cmg = act_ref[rows, c_off + g * SSD_STATE:c_off + (g + 1) * SSD_STATE]
            xdtg = xdt_ref[rows, xl:xl + SSD_GWP]
            cb = _dot_nt(cmg, bmg)
            yd = None
            for r in range(SSD_HPG):
                hd = g * SSD_HPG + r
                diff = cum_b[:, hd * LANES:hd * LANES + CHUNK] - cum_t[hd:hd + 1, 0:CHUNK]
                seg = jnp.exp(jnp.where(causal, diff, -jnp.inf))
                yr = _dot(cb * seg, xdtg)
                yd = yr if r == 0 else jnp.where(lane >= r * SSD_HEAD_DIM, yr, yd)
            hst = state_ref[g]
            y_off = _dot(cmg, hst) * ecum[:, xl:xl + SSD_GWP]
            state_ref[g] = hst * elast[:, xl:xl + SSD_GWP] + _dot_tn(bmg, xdtg * dec[:, xl:xl + SSD_GWP])
            y = yd + y_off + act_ref[rows, xl:xl + SSD_GWP] * dskip_ref[:, xl:xl + SSD_GWP]
            y = y * _silu(z_ref[rows, xl:xl + SSD_GWP])
            ms = jnp.sum(y * y, axis=-1, keepdims=True) * (1.0 / SSD_GW)
            y = y * lax.rsqrt(ms + EPS) * nw_ref[:, xl:xl + SSD_GWP]
            o_ref[0, rows, xl:xl + SSD_GWP] = y.astype(o_ref.dtype)
        return carry

    lax.fori_loop(0, t_blk // CHUNK, chunk, 0)


def _ssd_call(u, w_ssd, conv_w, conv_b, dt_bias, alog_c, alog_e, dskip_e, nw_e, t_blk):
    b, s, d = u.shape
    heads = jnp.arange(SSD_HEADS)
    lanes_e = (heads // SSD_HPG) * SSD_GWP + (heads % SSD_HPG) * SSD_HEAD_DIM
    col_e = jnp.arange(D_SSD_P)[None, :]
    expand = ((col_e >= lanes_e[:, None]) & (col_e < lanes_e[:, None] + SSD_HEAD_DIM))
    expand = jnp.pad(expand, ((0, LANES - SSD_HEADS), (0, 0))).astype(BF16)
    col_b = jnp.arange(SSD_HEADS * LANES)[None, :]
    expand128 = (col_b // LANES == heads[:, None])
    expand128 = jnp.pad(expand128, ((0, LANES - SSD_HEADS), (0, 0))).astype(BF16)

    full = lambda shape: pl.BlockSpec(shape, lambda bi, si: (0,) * len(shape))
    return pl.pallas_call(
        functools.partial(_ssd_kernel, t_blk=t_blk),
        grid=(b, s // t_blk),
        in_specs=[pl.BlockSpec((1, t_blk, d), lambda bi, si: (bi, si, 0)),
                  full((d, SSD_COLS)),
                  full((SSD_CONV, SSD_XBC_P)), full((1, SSD_XBC_P)),
                  full((1, LANES)), full((1, LANES)),
                  full((1, D_SSD_P)), full((1, D_SSD_P)), full((1, D_SSD_P)),
                  full((LANES, D_SSD_P)), full((LANES, SSD_HEADS * LANES))],
        out_specs=pl.BlockSpec((1, t_blk, D_SSD_P), lambda bi, si: (bi, si, 0)),
        out_shape=jax.ShapeDtypeStruct((b, s, D_SSD_P), BF16),
        scratch_shapes=[pltpu.VMEM((t_blk + CONV_HALO, SSD_XBC_P), F32),
                        pltpu.VMEM((t_blk, SSD_XBC_P), F32),
                        pltpu.VMEM((t_blk, D_SSD_P), F32),
                        pltpu.VMEM((t_blk, D_SSD_P), F32),
                        pltpu.VMEM((t_blk, D_SSD_P), F32),
                        pltpu.VMEM((t_blk, LANES), F32),
                        pltpu.VMEM((SSD_GROUPS, SSD_STATE, SSD_GWP), F32)],
        compiler_params=pltpu.CompilerParams(
            dimension_semantics=("parallel", "arbitrary"),
            vmem_limit_bytes=VMEM_LIMIT),
        name="ssd_mixer",
    )(u, w_ssd, conv_w, conv_b, dt_bias, alog_c, alog_e, dskip_e, nw_e, expand, expand128)


def _out_kernel(h_ref, ohg_ref, opool_ref, ossd_ref, p_ref, w1_ref, w2_ref, w3_ref, wpg_ref, wpe_ref,
                nw_ref, *out_refs, emit_h):
    dot = lambda a, w: jnp.dot(a, w, preferred_element_type=F32)
    acc = h_ref[...] + dot(ohg_ref[...], w1_ref[...]) + dot(opool_ref[...], w2_ref[...]) \
        + dot(ossd_ref[...], w3_ref[...])
    gate = _sigmoid(_dot(acc, wpg_ref[...]))
    hn = acc + gate * _dot(p_ref[...], wpe_ref[...])
    y = hn * lax.rsqrt(jnp.mean(hn * hn, axis=-1, keepdims=True) + EPS) * nw_ref[...]
    if emit_h:
        out_refs[0][...] = hn
        out_refs[1][...] = y.astype(out_refs[1].dtype)
    else:
        out_refs[0][...] = y


def _out_call(h2d, ohg, opool, ossd, p2d, w1, w2, w3, wpg, wpe, nw, tm, emit_h):
    n, d = h2d.shape
    rows = lambda width: pl.BlockSpec((tm, width), lambda i: (i, 0))
    full = lambda a: pl.BlockSpec(a.shape, lambda i: (0, 0))
    if emit_h:
        out_shape = (jax.ShapeDtypeStruct((n, d), F32), jax.ShapeDtypeStruct((n, d), BF16))
        out_specs = (rows(d), rows(d))
    else:
        out_shape = jax.ShapeDtypeStruct((n, d), F32)
        out_specs = rows(d)
    return pl.pallas_call(
        functools.partial(_out_kernel, emit_h=emit_h),
        grid=(n // tm,),
        in_specs=[rows(d), rows(D_HG), rows(D_POOL), rows(D_SSD_P), rows(P_DIM),
                  full(w1), full(w2), full(w3), full(wpg), full(wpe), pl.BlockSpec((1, d), lambda i: (0, 0))],
        out_specs=out_specs,
        out_shape=out_shape,
        compiler_params=pltpu.CompilerParams(
            dimension_semantics=("parallel",), vmem_limit_bytes=VMEM_LIMIT),
        name="out_proj",
    )(h2d, ohg, opool, ossd, p2d, w1, w2, w3, wpg, wpe, nw.reshape(1, d))


def _pad_groups(a):
    lead = a.shape[:-1]
    a = a.reshape(lead + (SSD_GROUPS, SSD_GW))
    a = jnp.pad(a, [(0, 0)] * len(lead) + [(0, 0), (0, SSD_GWP - SSD_GW)])
    return a.reshape(lead + (D_SSD_P,))


def _pad_lanes(a):
    return jnp.pad(a, [(0, 0)] * (a.ndim - 1) + [(0, LANES - a.shape[-1])])


def kernel(x, p, norm_w, w_in, hg_lb, hg_norm_w, pool_w, pool_scale, conv_w, conv_b, dt_bias, a_log, d_skip,
           ssd_norm_w, w_out, w_pe, w_pg, final_norm_w):
    b, s, d = x.shape
    depth = w_in.shape[0]
    n = b * s
    t_blk = min(256, s)
    tm = min(512, n)

    lb_all = jnp.cumsum(jax.nn.softmax(hg_lb.astype(F32), axis=0), axis=0)
    lb_all = lb_all - lb_all[0]

    o_hg0, o_pool0 = 0, D_HG
    o_ssd0 = D_HG + D_POOL
    c_pool = 4 * D_HG
    c_xbc = c_pool + 2 * D_POOL
    c_dt = c_xbc + D_SSD + 2 * SSD_GROUPS * SSD_STATE
    c_z = c_dt + SSD_HEADS

    h = x.reshape(n, d)
    u = _rms_norm_call(h, norm_w[0], tm).reshape(b, s, d)
    out = None
    for i in range(depth):
        w = w_in[i]
        w_hg = w[:, :c_pool].reshape(d, 4, HG_HEADS, HG_D).transpose(2, 0, 1, 3).reshape(HG_HEADS, d, 4 * HG_D)
        lb = lb_all[i].reshape(HG_HEADS, 1, HG_D)
        o_hg = _hgrn_call(u, w_hg.astype(BF16), jnp.log(lb), jnp.log1p(-lb),
                          hg_norm_w[i].reshape(HG_HEADS, 1, HG_D), t_blk)
        o_pool = _pool_call(u, w[:, c_pool:c_xbc].astype(BF16), pool_w[i].astype(BF16),
                            pool_scale[i].reshape(1, D_POOL), t_blk)
        xbc_split = lambda a: jnp.concatenate([_pad_groups(a[..., :D_SSD]), a[..., D_SSD:]], axis=-1)
        w_ssd = jnp.concatenate([xbc_split(w[:, c_xbc:c_dt]), _pad_groups(w[:, c_z:]), _pad_lanes(w[:, c_dt:c_z])],
                                axis=-1).astype(BF16)
        rep = lambda a: _pad_groups(jnp.repeat(a, SSD_HEAD_DIM)).reshape(1, D_SSD_P)
        o_ssd = _ssd_call(u, w_ssd, xbc_split(conv_w[i]), xbc_split(conv_b[i]).reshape(1, SSD_XBC_P),
                          _pad_lanes(dt_bias[i]).reshape(1, LANES), _pad_lanes(a_log[i]).reshape(1, LANES),
                          rep(a_log[i]), rep(d_skip[i]), _pad_groups(ssd_norm_w[i]).reshape(1, D_SSD_P), t_blk)
        wo = w_out[i]
        w3 = jnp.pad(wo[o_ssd0:].reshape(SSD_GROUPS, SSD_GW, d), ((0, 0), (0, SSD_GWP - SSD_GW), (0, 0)))
        last = i == depth - 1
        nw_next = final_norm_w if last else norm_w[i + 1]
        res = _out_call(h, o_hg.reshape(n, D_HG), o_pool.reshape(n, D_POOL), o_ssd.reshape(n, D_SSD_P),
                        p[i].reshape(n, P_DIM), wo[o_hg0:o_pool0].astype(BF16), wo[o_pool0:o_ssd0].astype(BF16),
                        w3.reshape(D_SSD_P, d).astype(BF16), w_pg[i].astype(BF16), w_pe[i].astype(BF16),
                        nw_next, tm, emit_h=not last)
        if last:
            out = res
        else:
            h, u = res
            u = u.reshape(b, s, d)
    return out.reshape(b, s, d)
```

```python
import functools

import jax
import jax.numpy as jnp
from jax import lax
from jax.experimental import pallas as pl
from jax.experimental.pallas import tpu as pltpu

F32 = jnp.float32
BF16 = jnp.bfloat16

CHUNK = 64
EPS = 1e-6
D_MODEL = 1024
P_DIM = 256
HG_HEADS = 6
HG_D = 128
HG_PAIR = 2
HG_LEVELS = (64, 32, 16, 8, 4)
HG_UNROLL = 4
LOG2E = 1.4426950408889634
D_HG = HG_HEADS * HG_D
POOL_WINDOWS = (2, 4, 8, 16)
POOL_CH = 128
D_POOL = len(POOL_WINDOWS) * POOL_CH
POOL_HALO = 16
SSD_HEADS = 12
SSD_HEAD_DIM = 64
SSD_GROUPS = 4
SSD_HPG = SSD_HEADS // SSD_GROUPS
SSD_GW = SSD_HPG * SSD_HEAD_DIM
SSD_GWP = 256
D_SSD = SSD_HEADS * SSD_HEAD_DIM
D_SSD_P = SSD_GROUPS * SSD_GWP
SSD_STATE = 128
SSD_CONV = 4
CONV_HALO = 8
LANES = 128
SSD_XBC_P = D_SSD_P + 2 * SSD_GROUPS * SSD_STATE
SSD_COLS = SSD_XBC_P + D_SSD_P + LANES
VMEM_LIMIT = 48 * 1024 * 1024


def _dot(a, b):
    return jnp.dot(a.astype(BF16), b.astype(BF16), preferred_element_type=F32)


def _dot_nt(a, b):
    return lax.dot_general(a.astype(BF16), b.astype(BF16), (((1,), (1,)), ((), ())),
                           preferred_element_type=F32)


def _dot_tn(a, b):
    return lax.dot_general(a.astype(BF16), b.astype(BF16), (((0,), (0,)), ((), ())),
                           preferred_element_type=F32)


def _split3(x):
    hi = x.astype(BF16)
    r1 = x - hi.astype(F32)
    mid = r1.astype(BF16)
    lo = (r1 - mid.astype(F32)).astype(BF16)
    return hi, mid, lo


def _sel_left(m, x):
    hi, mid, lo = _split3(x)
    d = lambda p: jnp.dot(m, p, preferred_element_type=F32)
    return d(hi) + (d(mid) + d(lo))


def _sel_right(x, m):
    hi, mid, lo = _split3(x)
    d = lambda p: jnp.dot(p, m, preferred_element_type=F32)
    return d(hi) + (d(mid) + d(lo))


def _sigmoid(x):
    return 1.0 / (1.0 + jnp.exp(-x))


def _silu(x):
    return x * _sigmoid(x)


def _softplus(x):
    return jnp.maximum(x, 0.0) + jnp.log1p(jnp.exp(-jnp.abs(x)))


def _tril_bf16(n):
    r = lax.broadcasted_iota(jnp.int32, (n, n), 0)
    c = lax.broadcasted_iota(jnp.int32, (n, n), 1)
    return (r >= c).astype(BF16)


def _norm_kernel(x_ref, w_ref, o_ref):
    x = x_ref[...]
    y = x * lax.rsqrt(jnp.mean(x * x, axis=-1, keepdims=True) + EPS)
    o_ref[...] = (y * w_ref[...]).astype(o_ref.dtype)


def _rms_norm_call(x2d, w, tm):
    n, d = x2d.shape
    return pl.pallas_call(
        _norm_kernel,
        grid=(n // tm,),
        in_specs=[pl.BlockSpec((tm, d), lambda i: (i, 0)),
                  pl.BlockSpec((1, d), lambda i: (0, 0))],
        out_specs=pl.BlockSpec((tm, d), lambda i: (i, 0)),
        out_shape=jax.ShapeDtypeStruct((n, d), BF16),
        compiler_params=pltpu.CompilerParams(dimension_semantics=("parallel",)),
        name="rms_norm_in",
    )(x2d, w.reshape(1, d))


def _hgrn_kernel(u_ref, wq_ref, wf_ref, wv_ref, wg_ref, loglb_ref, log1mlb_ref, nw_ref, o_ref,
                 proj_ref, cum_ref, pair_ref, state_ref, *, n_chunks):
    @pl.when(pl.program_id(2) == 0)
    def _():
        state_ref[...] = jnp.zeros_like(state_ref)

    pw = HG_PAIR * HG_D
    u = u_ref[0]
    for kind, w_ref in enumerate((wq_ref, wf_ref, wv_ref, wg_ref)):
        proj_ref[:, kind * pw:(kind + 1) * pw] = jnp.dot(u, w_ref[...], preferred_element_type=F32)

    row = lax.broadcasted_iota(jnp.int32, (CHUNK, HG_D), 0)
    sub = lax.broadcasted_iota(jnp.int32, (CHUNK // 8, 8, HG_D), 1)
    r64 = lax.broadcasted_iota(jnp.int32, (CHUNK, CHUNK), 0)
    c64 = lax.broadcasted_iota(jnp.int32, (CHUNK, CHUNK), 1)
    tril = (r64 >= c64).astype(BF16)
    level_mask = {bs: ((r64 // bs) == (c64 // bs)) & ((r64 & (bs // 2)) != 0) & ((c64 & (bs // 2)) == 0)
                  for bs in HG_LEVELS}
    diag0 = r64 == c64
    diag1 = (r64 == c64 + 1) & ((r64 & 1) == 1)
    odd = (row & 1) == 1

    def ref_rows(cum, bs):
        half = bs // 2
        if bs >= 8:
            c3 = cum.reshape(CHUNK // bs, bs, HG_D)
            return jnp.broadcast_to(c3[:, half - 1:half, :], c3.shape).reshape(CHUNK, HG_D)
        c3 = cum.reshape(CHUNK // 8, 8, HG_D)
        lo = jnp.broadcast_to(c3[:, 1:2, :], c3.shape)
        hi = jnp.broadcast_to(c3[:, 5:6, :], c3.shape)
        return jnp.where(sub < 4, lo, hi).reshape(CHUNK, HG_D)

    def gates(c, carry):
        rows = pl.ds(pl.multiple_of(c * CHUNK, CHUNK), CHUNK)
        for hh in range(HG_PAIR):
            lanes = slice(hh * HG_D, (hh + 1) * HG_D)
            z = proj_ref[rows, pw + hh * HG_D:pw + (hh + 1) * HG_D]
            loglb = loglb_ref[:, lanes]
            log1mlb = log1mlb_ref[:, lanes]
            log_sig = jnp.minimum(z, 0.0) - jnp.log1p(jnp.exp(-jnp.abs(z)))
            b_ = log1mlb + log_sig
            log_f = jnp.maximum(loglb, b_) + jnp.log1p(jnp.exp(-jnp.abs(loglb - b_)))
            f = jnp.exp(log_f)
            kk = 1.0 - f
            proj_ref[rows, pw + hh * HG_D:pw + (hh + 1) * HG_D] = kk
            cum_ref[rows, lanes] = _sel_left(tril, log_f * LOG2E)
            pair_ref[rows, lanes] = jnp.where(odd, pltpu.roll(kk, 1, 0) * f, 0.0)
        return carry

    lax.fori_loop(0, n_chunks, gates, 0, unroll=HG_UNROLL)

    def trip(c, carry):
        chains = [(j, hh) for j in range(HG_UNROLL) for hh in range(HG_PAIR)]
        rows = [pl.ds(pl.multiple_of((c * HG_UNROLL + j) * CHUNK, CHUNK), CHUNK) for j in range(HG_UNROLL)]
        lanes = [slice(hh * HG_D, (hh + 1) * HG_D) for hh in range(HG_PAIR)]
        col = lambda kind, hh: slice(kind * pw + hh * HG_D, kind * pw + (hh + 1) * HG_D)
        q = {ch: proj_ref[rows[ch[0]], col(0, ch[1])] for ch in chains}
        kk = {ch: proj_ref[rows[ch[0]], col(1, ch[1])] for ch in chains}
        cum = {ch: cum_ref[rows[ch[0]], lanes[ch[1]]] for ch in chains}

        scores = {}
        for ch in chains:
            acc = None
            for bs in HG_LEVELS:
                e = jnp.exp2(-jnp.abs(cum[ch] - ref_rows(cum[ch], bs)))
                m = jnp.where(level_mask[bs], _dot_nt(q[ch] * e, kk[ch] * e), 0.0)
                acc = m if acc is None else acc + m
            rs0 = jnp.sum(q[ch] * kk[ch], axis=1, keepdims=True)
            rs1 = jnp.sum(q[ch] * pair_ref[rows[ch[0]], lanes[ch[1]]], axis=1, keepdims=True)
            scores[ch] = acc + jnp.where(diag0, rs0, 0.0) + jnp.where(diag1, rs1, 0.0)

        inter = {}
        for ch in chains:
            j, hh = ch
            last = cum[ch][CHUNK - 1:CHUNK, :]
            st = state_ref[hh]
            inter[ch] = _dot_nt(q[ch] * jnp.exp2(cum[ch]), st)
            v = proj_ref[rows[j], col(2, hh)]
            state_ref[hh] = st * jnp.exp2(last) + _dot_tn(v, kk[ch] * jnp.exp2(last - cum[ch]))

        for ch in chains:
            j, hh = ch
            o = inter[ch] + _dot(scores[ch], proj_ref[rows[j], col(2, hh)])
            o = o * lax.rsqrt(jnp.mean(o * o, axis=-1, keepdims=True) + EPS) * nw_ref[:, lanes[hh]]
            g = proj_ref[rows[j], col(3, hh)]
            o_ref[0, rows[j], lanes[hh]] = (o * _silu(g)).astype(o_ref.dtype)
        return carry

    lax.fori_loop(0, n_chunks // HG_UNROLL, trip, 0)


def _hgrn_call(u, w_hg, loglb, log1mlb, nw, t_blk):
    b, s, d = u.shape
    n_pairs = HG_HEADS // HG_PAIR
    pw = HG_PAIR * HG_D
    grid = (b, n_pairs, s // t_blk)
    vec = pl.BlockSpec((1, pw), lambda bi, hp, si: (0, hp))
    wspec = lambda kind: pl.BlockSpec((d, pw), lambda bi, hp, si: (0, kind * n_pairs + hp))
    return pl.pallas_call(
        functools.partial(_hgrn_kernel, n_chunks=t_blk // CHUNK),
        grid=grid,
        in_specs=[pl.BlockSpec((1, t_blk, d), lambda bi, hp, si: (bi, si, 0)),
                  wspec(0), wspec(1), wspec(2), wspec(3),
                  vec, vec, vec],
        out_specs=pl.BlockSpec((1, t_blk, pw), lambda bi, hp, si: (bi, si, hp)),
        out_shape=jax.ShapeDtypeStruct((b, s, D_HG), BF16),
        scratch_shapes=[pltpu.VMEM((t_blk, 4 * pw), F32),
                        pltpu.VMEM((t_blk, pw), F32),
                        pltpu.VMEM((t_blk, pw), F32),
                        pltpu.VMEM((HG_PAIR, HG_D, HG_D), F32)],
        compiler_params=pltpu.CompilerParams(
            dimension_semantics=("parallel", "parallel", "arbitrary"),
            vmem_limit_bytes=VMEM_LIMIT),
        name="hgrn2_mixer",
    )(u, w_hg, w_hg, w_hg, w_hg, loglb, log1mlb, nw)


def _pool_kernel(u_ref, w_ref, pw_ref, ps_ref, o_ref, ext_ref, *, t_blk):
    si = pl.program_id(1)

    @pl.when(si == 0)
    def _():
        ext_ref[0:POOL_HALO, :] = jnp.zeros((POOL_HALO, D_POOL), F32)

    @pl.when(si > 0)
    def _():
        ext_ref[0:POOL_HALO, :] = ext_ref[t_blk:t_blk + POOL_HALO, :]

    proj = jnp.dot(u_ref[0], w_ref[...], preferred_element_type=F32)
    ext_ref[POOL_HALO:POOL_HALO + t_blk, :] = proj[:, :D_POOL]

    pos = (si * t_blk + 1 + lax.broadcasted_iota(jnp.int32, (t_blk, POOL_CH), 0)).astype(F32)
    for gi, win in enumerate(POOL_WINDOWS):
        lo = gi * POOL_CH
        acc = ext_ref[:, lo:lo + POOL_CH]
        shift = 1
        while shift < win:
            acc = acc + pltpu.roll(acc, shift, 0)
            shift *= 2
        cur = proj[:, lo:lo + POOL_CH]
        pooled = acc[POOL_HALO:, :] / jnp.minimum(pos, float(win)) - cur
        y = _dot(pooled, pw_ref[gi]) * ps_ref[:, lo:lo + POOL_CH]
        gate = proj[:, D_POOL + lo:D_POOL + lo + POOL_CH]
        o_ref[0, :, lo:lo + POOL_CH] = (y * _silu(gate)).astype(o_ref.dtype)


def _pool_call(u, w_pool, pool_w, pool_scale, t_blk):
    b, s, d = u.shape
    return pl.pallas_call(
        functools.partial(_pool_kernel, t_blk=t_blk),
        grid=(b, s // t_blk),
        in_specs=[pl.BlockSpec((1, t_blk, d), lambda bi, si: (bi, si, 0)),
                  pl.BlockSpec((d, 2 * D_POOL), lambda bi, si: (0, 0)),
                  pl.BlockSpec((len(POOL_WINDOWS), POOL_CH, POOL_CH), lambda bi, si: (0, 0, 0)),
                  pl.BlockSpec((1, D_POOL), lambda bi, si: (0, 0))],
        out_specs=pl.BlockSpec((1, t_blk, D_POOL), lambda bi, si: (bi, si, 0)),
        out_shape=jax.ShapeDtypeStruct((b, s, D_POOL), BF16),
        scratch_shapes=[pltpu.VMEM((t_blk + POOL_HALO, D_POOL), F32)],
        compiler_params=pltpu.CompilerParams(
            dimension_semantics=("parallel", "arbitrary"),
            vmem_limit_bytes=VMEM_LIMIT),
        name="pool_mixer",
    )(u, w_pool, pool_w, pool_scale)


def _ssd_kernel(u_ref, w_ref, cw_ref, cb_ref, dtb_ref, alogc_ref, dskip_ref, nw_ref,
                exp_ref, btril_ref, dup_ref, o_ref,
                ext_ref, act_ref, z_ref, xdt_ref, cume_ref, cumt_ref, state_ref, *, t_blk):
    si = pl.program_id(1)
    n_chunks = t_blk // CHUNK

    @pl.when(si == 0)
    def _():
        ext_ref[0:CONV_HALO, :] = jnp.zeros((CONV_HALO, SSD_XBC_P), F32)
        state_ref[...] = jnp.zeros_like(state_ref)

    @pl.when(si > 0)
    def _():
        ext_ref[0:CONV_HALO, :] = ext_ref[t_blk:t_blk + CONV_HALO, :]

    proj = jnp.dot(u_ref[0], w_ref[...], preferred_element_type=F32)
    ext_ref[CONV_HALO:CONV_HALO + t_blk, :] = proj[:, :SSD_XBC_P]
    z_ref[...] = proj[:, SSD_XBC_P:SSD_XBC_P + D_SSD_P]

    e = ext_ref[...]
    xc = cw_ref[3:4, :] * e[CONV_HALO:, :] + cb_ref[...]
    for j in range(1, SSD_CONV):
        xc = xc + cw_ref[3 - j:4 - j, :] * pltpu.roll(e, j, 0)[CONV_HALO:, :]
    xc = _silu(xc)
    act_ref[...] = xc

    dt = _softplus(proj[:, SSD_XBC_P + D_SSD_P:] + dtb_ref[...])
    xdt_ref[...] = xc[:, :D_SSD_P] * _sel_right(dt, exp_ref[...])

    cum_c = _sel_left(btril_ref[...], dt * (-jnp.exp(alogc_ref[...]) * LOG2E))
    hi, mid, lo = _split3(cum_c)
    cume_ref[...] = jnp.dot(hi, exp_ref[...], preferred_element_type=F32) + (
        jnp.dot(mid, exp_ref[...], preferred_element_type=F32)
        + jnp.dot(lo, exp_ref[...], preferred_element_type=F32))
    tn = lambda part: lax.dot_general(part, dup_ref[...], (((0,), (0,)), ((), ())), preferred_element_type=F32)
    cum_t = tn(hi) + (tn(mid) + tn(lo))
    for c in range(n_chunks):
        cumt_ref[c] = cum_t[:, c * LANES:(c + 1) * LANES]

    r64 = lax.broadcasted_iota(jnp.int32, (CHUNK, CHUNK), 0)
    c64 = lax.broadcasted_iota(jnp.int32, (CHUNK, CHUNK), 1)
    causal = r64 >= c64
    r128 = lax.broadcasted_iota(jnp.int32, (CHUNK, LANES), 0)
    l128 = lax.broadcasted_iota(jnp.int32, (CHUNK, LANES), 1)
    causal2 = r128 >= (l128 & (CHUNK - 1))
    first_half = lax.broadcasted_iota(jnp.int32, (1, LANES), 1) < CHUNK
    rb = lax.broadcasted_iota(jnp.int32, (LANES, LANES), 0)
    lb = lax.broadcasted_iota(jnp.int32, (LANES, LANES), 1)
    blockdiag = (rb >= CHUNK) == (lb >= CHUNK)
    b_off = D_SSD_P
    c_off = D_SSD_P + SSD_GROUPS * SSD_STATE
    groups = range(SSD_GROUPS)

    def chunk(c, carry):
        rows = pl.ds(pl.multiple_of(c * CHUNK, CHUNK), CHUNK)
        gl = [slice(g * SSD_GWP, (g + 1) * SSD_GWP) for g in groups]
        bm = [act_ref[rows, b_off + g * SSD_STATE:b_off + (g + 1) * SSD_STATE] for g in groups]
        cm = [act_ref[rows, c_off + g * SSD_STATE:c_off + (g + 1) * SSD_STATE] for g in groups]
        xdt = [xdt_ref[rows, gl[g]] for g in groups]
        cum_e = cume_ref[rows, :]
        last_e = cum_e[CHUNK - 1:CHUNK, :]
        cum_t = cumt_ref[c]

        cb2 = [_dot_nt(cm[g], jnp.concatenate([bm[g], bm[g]], axis=0)) for g in groups]
        y_diag = []
        for g in groups:
            hd = g * SSD_HPG
            cs01 = jnp.where(first_half, cum_t[hd:hd + 1, :], cum_t[hd + 1:hd + 2, :])
            ct01 = cum_e[:, g * SSD_GWP:g * SSD_GWP + LANES]
            seg01 = jnp.exp2(jnp.where(causal2, ct01 - cs01, -jnp.inf))
            ct2 = cum_e[:, g * SSD_GWP + LANES:g * SSD_GWP + LANES + CHUNK]
            seg2 = jnp.exp2(jnp.where(causal, ct2 - cum_t[hd + 2:hd + 3, 0:CHUNK], -jnp.inf))
            x01 = xdt[g][:, :LANES]
            xbd = jnp.where(blockdiag, jnp.concatenate([x01, x01], axis=0), 0.0)
            y01 = _dot(cb2[g] * seg01, xbd)
            y2 = _dot(cb2[g][:, :CHUNK] * seg2, xdt[g][:, LANES:])
            y_diag.append(jnp.concatenate([y01, y2], axis=1))

        y_off = []
        for g in groups:
            hst = state_ref[g]
            y_off.append(_dot(cm[g], hst) * jnp.exp2(cum_e[:, gl[g]]))
            dec = jnp.exp2(last_e[:, gl[g]] - cum_e[:, gl[g]])
            state_ref[g] = hst * jnp.exp2(last_e[:, gl[g]]) + _dot_tn(bm[g], xdt[g] * dec)

        for g in groups:
            y = y_diag[g] + y_off[g] + act_ref[rows, gl[g]] * dskip_ref[:, gl[g]]
            y = y * _silu(z_ref[rows, gl[g]])
            ms = jnp.sum(y * y, axis=-1, keepdims=True) * (1.0 / SSD_GW)
            y = y * lax.rsqrt(ms + EPS) * nw_ref[:, gl[g]]
            o_ref[0, rows, gl[g]] = y.astype(o_ref.dtype)
        return carry

    lax.fori_loop(0, n_chunks, chunk, 0)


def _ssd_call(u, w_ssd, conv_w, conv_b, dt_bias, alog_c, dskip_e, nw_e, t_blk):
    b, s, d = u.shape
    n_chunks = t_blk // CHUNK
    heads = jnp.arange(SSD_HEADS)
    lanes_e = (heads // SSD_HPG) * SSD_GWP + (heads % SSD_HPG) * SSD_HEAD_DIM
    col_e = jnp.arange(D_SSD_P)[None, :]
    expand = ((col_e >= lanes_e[:, None]) & (col_e < lanes_e[:, None] + SSD_HEAD_DIM))
    expand = jnp.pad(expand, ((0, LANES - SSD_HEADS), (0, 0))).astype(BF16)
    ti = jnp.arange(t_blk)
    btril = ((ti[:, None] >= ti[None, :]) & (ti[:, None] // CHUNK == ti[None, :] // CHUNK)).astype(BF16)
    cj = jnp.arange(2 * t_blk)
    dup = (ti[:, None] == (cj[None, :] // LANES) * CHUNK + cj[None, :] % CHUNK).astype(BF16)

    full = lambda shape: pl.BlockSpec(shape, lambda bi, si: (0,) * len(shape))
    return pl.pallas_call(
        functools.partial(_ssd_kernel, t_blk=t_blk),
        grid=(b, s // t_blk),
        in_specs=[pl.BlockSpec((1, t_blk, d), lambda bi, si: (bi, si, 0)),
                  full((d, SSD_COLS)),
                  full((SSD_CONV, SSD_XBC_P)), full((1, SSD_XBC_P)),
                  full((1, LANES)), full((1, LANES)),
                  full((1, D_SSD_P)), full((1, D_SSD_P)),
                  full((LANES, D_SSD_P)), full((t_blk, t_blk)), full((t_blk, 2 * t_blk))],
        out_specs=pl.BlockSpec((1, t_blk, D_SSD_P), lambda bi, si: (bi, si, 0)),
        out_shape=jax.ShapeDtypeStruct((b, s, D_SSD_P), BF16),
        scratch_shapes=[pltpu.VMEM((t_blk + CONV_HALO, SSD_XBC_P), F32),
                        pltpu.VMEM((t_blk, SSD_XBC_P), F32),
                        pltpu.VMEM((t_blk, D_SSD_P), F32),
                        pltpu.VMEM((t_blk, D_SSD_P), F32),
                        pltpu.VMEM((t_blk, D_SSD_P), F32),
                        pltpu.VMEM((n_chunks, LANES, LANES), F32),
                        pltpu.VMEM((SSD_GROUPS, SSD_STATE, SSD_GWP), F32)],
        compiler_params=pltpu.CompilerParams(
            dimension_semantics=("parallel", "arbitrary"),
            vmem_limit_bytes=VMEM_LIMIT),
        name="ssd_mixer",
    )(u, w_ssd, conv_w, conv_b, dt_bias, alog_c, dskip_e, nw_e, expand, btril, dup)


def _out_kernel(h_ref, ohg_ref, opool_ref, ossd_ref, p_ref, w1_ref, w2_ref, w3_ref, wpg_ref, wpe_ref,
                nw_ref, *out_refs, emit_h):
    dot = lambda a, w: jnp.dot(a, w, preferred_element_type=F32)
    acc = h_ref[...] + dot(ohg_ref[...], w1_ref[...]) + dot(opool_ref[...], w2_ref[...]) \
        + dot(ossd_ref[...], w3_ref[...])
    gate = _sigmoid(_dot(acc, wpg_ref[...]))
    hn = acc + gate * _dot(p_ref[...], wpe_ref[...])
    y = hn * lax.rsqrt(jnp.mean(hn * hn, axis=-1, keepdims=True) + EPS) * nw_ref[...]
    if emit_h:
        out_refs[0][...] = hn
        out_refs[1][...] = y.astype(out_refs[1].dtype)
    else:
        out_refs[0][...] = y


def _out_call(h2d, ohg, opool, ossd, p2d, w1, w2, w3, wpg, wpe, nw, tm, emit_h):
    n, d = h2d.shape
    rows = lambda width: pl.BlockSpec((tm, width), lambda i: (i, 0))
    full = lambda a: pl.BlockSpec(a.shape, lambda i: (0, 0))
    if emit_h:
        out_shape = (jax.ShapeDtypeStruct((n, d), F32), jax.ShapeDtypeStruct((n, d), BF16))
        out_specs = (rows(d), rows(d))
    else:
        out_shape = jax.ShapeDtypeStruct((n, d), F32)
        out_specs = rows(d)
    return pl.pallas_call(
        functools.partial(_out_kernel, emit_h=emit_h),
        grid=(n // tm,),
        in_specs=[rows(d), rows(D_HG), rows(D_POOL), rows(D_SSD_P), rows(P_DIM),
                  full(w1), full(w2), full(w3), full(wpg), full(wpe), pl.BlockSpec((1, d), lambda i: (0, 0))],
        out_specs=out_specs,
        out_shape=out_shape,
        compiler_params=pltpu.CompilerParams(
            dimension_semantics=("parallel",), vmem_limit_bytes=VMEM_LIMIT),
        name="out_proj",
    )(h2d, ohg, opool, ossd, p2d, w1, w2, w3, wpg, wpe, nw.reshape(1, d))


def _pad_groups(a):
    lead = a.shape[:-1]
    a = a.reshape(lead + (SSD_GROUPS, SSD_GW))
    a = jnp.pad(a, [(0, 0)] * len(lead) + [(0, 0), (0, SSD_GWP - SSD_GW)])
    return a.reshape(lead + (D_SSD_P,))


def _pad_lanes(a):
    return jnp.pad(a, [(0, 0)] * (a.ndim - 1) + [(0, LANES - a.shape[-1])])


def kernel(x, p, norm_w, w_in, hg_lb, hg_norm_w, pool_w, pool_scale, conv_w, conv_b, dt_bias, a_log, d_skip,
           ssd_norm_w, w_out, w_pe, w_pg, final_norm_w):
    b, s, d = x.shape
    depth = w_in.shape[0]
    n = b * s
    t_blk = min(256, s)
    t_hg = min(512, s)
    tm = min(512, n)

    lb_all = jnp.cumsum(jax.nn.softmax(hg_lb.astype(F32), axis=0), axis=0)
    lb_all = lb_all - lb_all[0]

    o_hg0, o_pool0 = 0, D_HG
    o_ssd0 = D_HG + D_POOL
    c_pool = 4 * D_HG
    c_xbc = c_pool + 2 * D_POOL
    c_dt = c_xbc + D_SSD + 2 * SSD_GROUPS * SSD_STATE
    c_z = c_dt + SSD_HEADS

    h = x.reshape(n, d)
    u = _rms_norm_call(h, norm_w[0], tm).reshape(b, s, d)
    out = None
    for i in range(depth):
        w = w_in[i]
        lb = lb_all[i].reshape(1, D_HG)
        o_hg = _hgrn_call(u, w[:, :c_pool].astype(BF16), jnp.log(lb), jnp.log1p(-lb),
                          hg_norm_w[i].reshape(1, D_HG), t_hg)
        o_pool = _pool_call(u, w[:, c_pool:c_xbc].astype(BF16), pool_w[i].astype(BF16),
                            pool_scale[i].reshape(1, D_POOL), t_blk)
        xbc_split = lambda a: jnp.concatenate([_pad_groups(a[..., :D_SSD]), a[..., D_SSD:]], axis=-1)
        w_ssd = jnp.concatenate([xbc_split(w[:, c_xbc:c_dt]), _pad_groups(w[:, c_z:]), _pad_lanes(w[:, c_dt:c_z])],
                                axis=-1).astype(BF16)
        rep = lambda a: _pad_groups(jnp.repeat(a, SSD_HEAD_DIM)).reshape(1, D_SSD_P)
        o_ssd = _ssd_call(u, w_ssd, xbc_split(conv_w[i]), xbc_split(conv_b[i]).reshape(1, SSD_XBC_P),
                          _pad_lanes(dt_bias[i]).reshape(1, LANES), _pad_lanes(a_log[i]).reshape(1, LANES),
                          rep(d_skip[i]), _pad_groups(ssd_norm_w[i]).reshape(1, D_SSD_P), t_blk)
        wo = w_out[i]
        w3 = jnp.pad(wo[o_ssd0:].reshape(SSD_GROUPS, SSD_GW, d), ((0, 0), (0, SSD_GWP - SSD_GW), (0, 0)))
        last = i == depth - 1
        nw_next = final_norm_w if last else norm_w[i + 1]
        res = _out_call(h, o_hg.reshape(n, D_HG), o_pool.reshape(n, D_POOL), o_ssd.reshape(n, D_SSD_P),
                        p[i].reshape(n, P_DIM), wo[o_hg0:o_pool0].astype(BF16), wo[o_pool0:o_ssd0].astype(BF16),
                        w3.reshape(D_SSD_P, d).astype(BF16), w_pg[i].astype(BF16), w_pe[i].astype(BF16),
                        nw_next, tm, emit_h=not last)
        if last:
            out = res
        else:
            h, u = res
            u = u.reshape(b, s, d)
    return out.reshape(b, s, d)
```

```python
import functools

import jax
import jax.numpy as jnp
from jax import lax
from jax.experimental import pallas as pl
from jax.experimental.pallas import tpu as pltpu

F32 = jnp.float32
BF16 = jnp.bfloat16

CHUNK = 64
EPS = 1e-6
D_MODEL = 1024
P_DIM = 256
HG_HEADS = 6
HG_D = 128
HG_PAIR = 2
HG_LEVELS = (64, 32, 16, 8, 4)
HG_UNROLL = 4
LOG2E = 1.4426950408889634
D_HG = HG_HEADS * HG_D
POOL_WINDOWS = (2, 4, 8, 16)
POOL_CH = 128
D_POOL = len(POOL_WINDOWS) * POOL_CH
POOL_HALO = 16
SSD_HEADS = 12
SSD_HEAD_DIM = 64
SSD_GROUPS = 4
SSD_HPG = SSD_HEADS // SSD_GROUPS
SSD_GW = SSD_HPG * SSD_HEAD_DIM
SSD_GWP = 256
D_SSD = SSD_HEADS * SSD_HEAD_DIM
D_SSD_P = SSD_GROUPS * SSD_GWP
SSD_STATE = 128
SSD_CONV = 4
CONV_HALO = 8
LANES = 128
SSD_XBC_P = D_SSD_P + 2 * SSD_GROUPS * SSD_STATE
SSD_COLS = SSD_XBC_P + D_SSD_P + LANES
VMEM_LIMIT = 48 * 1024 * 1024


def _dot(a, b):
    return jnp.dot(a.astype(BF16), b.astype(BF16), preferred_element_type=F32)


def _dot_nt(a, b):
    return lax.dot_general(a.astype(BF16), b.astype(BF16), (((1,), (1,)), ((), ())),
                           preferred_element_type=F32)


def _dot_tn(a, b):
    return lax.dot_general(a.astype(BF16), b.astype(BF16), (((0,), (0,)), ((), ())),
                           preferred_element_type=F32)


def _split3(x):
    hi = x.astype(BF16)
    r1 = x - hi.astype(F32)
    mid = r1.astype(BF16)
    lo = (r1 - mid.astype(F32)).astype(BF16)
    return hi, mid, lo


def _sel_left(m, x):
    hi, mid, lo = _split3(x)
    d = lambda p: jnp.dot(m, p, preferred_element_type=F32)
    return d(hi) + (d(mid) + d(lo))


def _sel_right(x, m):
    hi, mid, lo = _split3(x)
    d = lambda p: jnp.dot(p, m, preferred_element_type=F32)
    return d(hi) + (d(mid) + d(lo))


def _sel_left3(m3, x):
    return jnp.dot(m3, jnp.concatenate(_split3(x), axis=0), preferred_element_type=F32)


def _sel_right3(x, m3):
    return jnp.dot(jnp.concatenate(_split3(x), axis=1), m3, preferred_element_type=F32)


def _sigmoid(x):
    return 1.0 / (1.0 + jnp.exp(-x))


def _silu(x):
    return x * _sigmoid(x)


def _softplus(x):
    return jnp.maximum(x, 0.0) + jnp.log1p(jnp.exp(-jnp.abs(x)))


def _tril_bf16(n):
    r = lax.broadcasted_iota(jnp.int32, (n, n), 0)
    c = lax.broadcasted_iota(jnp.int32, (n, n), 1)
    return (r >= c).astype(BF16)


def _norm_kernel(x_ref, w_ref, o_ref):
    x = x_ref[...]
    y = x * lax.rsqrt(jnp.mean(x * x, axis=-1, keepdims=True) + EPS)
    o_ref[...] = (y * w_ref[...]).astype(o_ref.dtype)


def _rms_norm_call(x2d, w, tm):
    n, d = x2d.shape
    return pl.pallas_call(
        _norm_kernel,
        grid=(n // tm,),
        in_specs=[pl.BlockSpec((tm, d), lambda i: (i, 0)),
                  pl.BlockSpec((1, d), lambda i: (0, 0))],
        out_specs=pl.BlockSpec((tm, d), lambda i: (i, 0)),
        out_shape=jax.ShapeDtypeStruct((n, d), BF16),
        compiler_params=pltpu.CompilerParams(dimension_semantics=("parallel",)),
        name="rms_norm_in",
    )(x2d, w.reshape(1, d))


def _hgrn_kernel(u_ref, wq_ref, wf_ref, wv_ref, wg_ref, loglb_ref, log1mlb_ref, nw_ref, o_ref,
                 proj_ref, cum_ref, pair_ref, state_ref, *, n_chunks):
    @pl.when(pl.program_id(2) == 0)
    def _():
        state_ref[...] = jnp.zeros_like(state_ref)

    pw = HG_PAIR * HG_D
    u = u_ref[0]
    for kind, w_ref in enumerate((wq_ref, wf_ref, wv_ref, wg_ref)):
        proj_ref[:, kind * pw:(kind + 1) * pw] = jnp.dot(u, w_ref[...], preferred_element_type=F32)

    row = lax.broadcasted_iota(jnp.int32, (CHUNK, HG_D), 0)
    sub = lax.broadcasted_iota(jnp.int32, (CHUNK // 8, 8, HG_D), 1)
    r64 = lax.broadcasted_iota(jnp.int32, (CHUNK, CHUNK), 0)
    c64 = lax.broadcasted_iota(jnp.int32, (CHUNK, CHUNK), 1)
    r3 = lax.broadcasted_iota(jnp.int32, (CHUNK, 3 * CHUNK), 0)
    c3 = lax.broadcasted_iota(jnp.int32, (CHUNK, 3 * CHUNK), 1)
    tril3 = (r3 >= (c3 % CHUNK)).astype(BF16)
    level_mask = {bs: ((r64 // bs) == (c64 // bs)) & ((r64 & (bs // 2)) != 0) & ((c64 & (bs // 2)) == 0)
                  for bs in HG_LEVELS}
    diag0 = r64 == c64
    diag1 = (r64 == c64 + 1) & ((r64 & 1) == 1)
    odd = (row & 1) == 1

    def ref_rows(cum, bs):
        half = bs // 2
        if bs >= 8:
            c3 = cum.reshape(CHUNK // bs, bs, HG_D)
            return jnp.broadcast_to(c3[:, half - 1:half, :], c3.shape).reshape(CHUNK, HG_D)
        c3 = cum.reshape(CHUNK // 8, 8, HG_D)
        lo = jnp.broadcast_to(c3[:, 1:2, :], c3.shape)
        hi = jnp.broadcast_to(c3[:, 5:6, :], c3.shape)
        return jnp.where(sub < 4, lo, hi).reshape(CHUNK, HG_D)

    def gates(c, carry):
        rows = pl.ds(pl.multiple_of(c * CHUNK, CHUNK), CHUNK)
        for hh in range(HG_PAIR):
            lanes = slice(hh * HG_D, (hh + 1) * HG_D)
            z = proj_ref[rows, pw + hh * HG_D:pw + (hh + 1) * HG_D]
            loglb = loglb_ref[:, lanes]
            log1mlb = log1mlb_ref[:, lanes]
            log_sig = jnp.minimum(z, 0.0) - jnp.log1p(jnp.exp(-jnp.abs(z)))
            b_ = log1mlb + log_sig
            log_f = jnp.maximum(loglb, b_) + jnp.log1p(jnp.exp(-jnp.abs(loglb - b_)))
            f = jnp.exp(log_f)
            kk = 1.0 - f
            proj_ref[rows, pw + hh * HG_D:pw + (hh + 1) * HG_D] = kk
            cum_ref[rows, lanes] = _sel_left3(tril3, log_f * LOG2E)
            pair_ref[rows, lanes] = jnp.where(odd, pltpu.roll(kk, 1, 0) * f, 0.0)
        return carry

    lax.fori_loop(0, n_chunks, gates, 0, unroll=HG_UNROLL)

    def trip(c, carry):
        chains = [(j, hh) for j in range(HG_UNROLL) for hh in range(HG_PAIR)]
        rows = [pl.ds(pl.multiple_of((c * HG_UNROLL + j) * CHUNK, CHUNK), CHUNK) for j in range(HG_UNROLL)]
        lanes = [slice(hh * HG_D, (hh + 1) * HG_D) for hh in range(HG_PAIR)]
        col = lambda kind, hh: slice(kind * pw + hh * HG_D, kind * pw + (hh + 1) * HG_D)
        q = {ch: proj_ref[rows[ch[0]], col(0, ch[1])] for ch in chains}
        kk = {ch: proj_ref[rows[ch[0]], col(1, ch[1])] for ch in chains}
        cum = {ch: cum_ref[rows[ch[0]], lanes[ch[1]]] for ch in chains}

        scores = {}
        for ch in chains:
            acc = None
            for bs in HG_LEVELS:
                e = jnp.exp2(-jnp.abs(cum[ch] - ref_rows(cum[ch], bs)))
                m = jnp.where(level_mask[bs], _dot_nt(q[ch] * e, kk[ch] * e), 0.0)
                acc = m if acc is None else acc + m
            rs0 = jnp.sum(q[ch] * kk[ch], axis=1, keepdims=True)
            rs1 = jnp.sum(q[ch] * pair_ref[rows[ch[0]], lanes[ch[1]]], axis=1, keepdims=True)
            scores[ch] = acc + jnp.where(diag0, rs0, 0.0) + jnp.where(diag1, rs1, 0.0)

        inter = {}
        for ch in chains:
            j, hh = ch
            last = cum[ch][CHUNK - 1:CHUNK, :]
            st = state_ref[hh]
            inter[ch] = _dot_nt(q[ch] * jnp.exp2(cum[ch]), st)
            v = proj_ref[rows[j], col(2, hh)]
            state_ref[hh] = st * jnp.exp2(last) + _dot_tn(v, kk[ch] * jnp.exp2(last - cum[ch]))

        for ch in chains:
            j, hh = ch
            o = inter[ch] + _dot(scores[ch], proj_ref[rows[j], col(2, hh)])
            o = o * lax.rsqrt(jnp.mean(o * o, axis=-1, keepdims=True) + EPS) * nw_ref[:, lanes[hh]]
            g = proj_ref[rows[j], col(3, hh)]
            o_ref[0, rows[j], lanes[hh]] = (o * _silu(g)).astype(o_ref.dtype)
        return carry

    lax.fori_loop(0, n_chunks // HG_UNROLL, trip, 0)


def _hgrn_call(u, w_hg, loglb, log1mlb, nw, t_blk):
    b, s, d = u.shape
    n_pairs = HG_HEADS // HG_PAIR
    pw = HG_PAIR * HG_D
    grid = (b, n_pairs, s // t_blk)
    vec = pl.BlockSpec((1, pw), lambda bi, hp, si: (0, hp))
    wspec = lambda kind: pl.BlockSpec((d, pw), lambda bi, hp, si: (0, kind * n_pairs + hp))
    return pl.pallas_call(
        functools.partial(_hgrn_kernel, n_chunks=t_blk // CHUNK),
        grid=grid,
        in_specs=[pl.BlockSpec((1, t_blk, d), lambda bi, hp, si: (bi, si, 0)),
                  wspec(0), wspec(1), wspec(2), wspec(3),
                  vec, vec, vec],
        out_specs=pl.BlockSpec((1, t_blk, pw), lambda bi, hp, si: (bi, si, hp)),
        out_shape=jax.ShapeDtypeStruct((b, s, D_HG), BF16),
        scratch_shapes=[pltpu.VMEM((t_blk, 4 * pw), F32),
                        pltpu.VMEM((t_blk, pw), F32),
                        pltpu.VMEM((t_blk, pw), F32),
                        pltpu.VMEM((HG_PAIR, HG_D, HG_D), F32)],
        compiler_params=pltpu.CompilerParams(
            dimension_semantics=("parallel", "parallel", "arbitrary"),
            vmem_limit_bytes=VMEM_LIMIT),
        name="hgrn2_mixer",
    )(u, w_hg, w_hg, w_hg, w_hg, loglb, log1mlb, nw)


def _pool_kernel(u_ref, w_ref, pw_ref, ps_ref, o_ref, ext_ref, *, t_blk):
    si = pl.program_id(1)

    @pl.when(si == 0)
    def _():
        ext_ref[0:POOL_HALO, :] = jnp.zeros((POOL_HALO, D_POOL), F32)

    @pl.when(si > 0)
    def _():
        ext_ref[0:POOL_HALO, :] = ext_ref[t_blk:t_blk + POOL_HALO, :]

    proj = jnp.dot(u_ref[0], w_ref[...], preferred_element_type=F32)
    ext_ref[POOL_HALO:POOL_HALO + t_blk, :] = proj[:, :D_POOL]

    pos = (si * t_blk + 1 + lax.broadcasted_iota(jnp.int32, (t_blk, POOL_CH), 0)).astype(F32)
    for gi, win in enumerate(POOL_WINDOWS):
        lo = gi * POOL_CH
        acc = ext_ref[:, lo:lo + POOL_CH]
        shift = 1
        while shift < win:
            acc = acc + pltpu.roll(acc, shift, 0)
            shift *= 2
        cur = proj[:, lo:lo + POOL_CH]
        pooled = acc[POOL_HALO:, :] / jnp.minimum(pos, float(win)) - cur
        y = _dot(pooled, pw_ref[gi]) * ps_ref[:, lo:lo + POOL_CH]
        gate = proj[:, D_POOL + lo:D_POOL + lo + POOL_CH]
        o_ref[0, :, lo:lo + POOL_CH] = (y * _silu(gate)).astype(o_ref.dtype)


def _pool_call(u, w_pool, pool_w, pool_scale, t_blk):
    b, s, d = u.shape
    return pl.pallas_call(
        functools.partial(_pool_kernel, t_blk=t_blk),
        grid=(b, s // t_blk),
        in_specs=[pl.BlockSpec((1, t_blk, d), lambda bi, si: (bi, si, 0)),
                  pl.BlockSpec((d, 2 * D_POOL), lambda bi, si: (0, 0)),
                  pl.BlockSpec((len(POOL_WINDOWS), POOL_CH, POOL_CH), lambda bi, si: (0, 0, 0)),
                  pl.BlockSpec((1, D_POOL), lambda bi, si: (0, 0))],
        out_specs=pl.BlockSpec((1, t_blk, D_POOL), lambda bi, si: (bi, si, 0)),
        out_shape=jax.ShapeDtypeStruct((b, s, D_POOL), BF16),
        scratch_shapes=[pltpu.VMEM((t_blk + POOL_HALO, D_POOL), F32)],
        compiler_params=pltpu.CompilerParams(
            dimension_semantics=("parallel", "arbitrary"),
            vmem_limit_bytes=VMEM_LIMIT),
        name="pool_mixer",
    )(u, w_pool, pool_w, pool_scale)


def _ssd_kernel(u_ref, w_ref, cw_ref, cb_ref, dtb_ref, alogc_ref, dskip_ref, nw_ref,
                exp_ref, btril_ref, dup_ref, o_ref,
                ext_ref, act_ref, z_ref, xdt_ref, cume_ref, cumt_ref, state_ref, *, t_blk):
    si = pl.program_id(1)
    n_chunks = t_blk // CHUNK

    @pl.when(si == 0)
    def _():
        ext_ref[0:CONV_HALO, :] = jnp.zeros((CONV_HALO, SSD_XBC_P), F32)
        state_ref[...] = jnp.zeros_like(state_ref)

    @pl.when(si > 0)
    def _():
        ext_ref[0:CONV_HALO, :] = ext_ref[t_blk:t_blk + CONV_HALO, :]

    proj = jnp.dot(u_ref[0], w_ref[...], preferred_element_type=F32)
    ext_ref[CONV_HALO:CONV_HALO + t_blk, :] = proj[:, :SSD_XBC_P]
    z_ref[...] = proj[:, SSD_XBC_P:SSD_XBC_P + D_SSD_P]

    e = ext_ref[...]
    xc = cw_ref[3:4, :] * e[CONV_HALO:, :] + cb_ref[...]
    for j in range(1, SSD_CONV):
        xc = xc + cw_ref[3 - j:4 - j, :] * pltpu.roll(e, j, 0)[CONV_HALO:, :]
    xc = _silu(xc)
    act_ref[...] = xc

    dt = _softplus(proj[:, SSD_XBC_P + D_SSD_P:] + dtb_ref[...])
    exp3 = jnp.concatenate([exp_ref[...]] * 3, axis=0)
    xdt_ref[...] = xc[:, :D_SSD_P] * _sel_right3(dt, exp3)

    cum_c = _sel_left(btril_ref[...], dt * (-jnp.exp(alogc_ref[...]) * LOG2E))
    hi, mid, lo = _split3(cum_c)
    cume_ref[...] = jnp.dot(jnp.concatenate([hi, mid, lo], axis=1), exp3,
                            preferred_element_type=F32)
    cum_t = lax.dot_general(jnp.concatenate([hi, mid, lo], axis=0), jnp.concatenate([dup_ref[...]] * 3, axis=0),
                            (((0,), (0,)), ((), ())), preferred_element_type=F32)
    for c in range(n_chunks):
        cumt_ref[c] = cum_t[:, c * LANES:(c + 1) * LANES]

    r64 = lax.broadcasted_iota(jnp.int32, (CHUNK, CHUNK), 0)
    c64 = lax.broadcasted_iota(jnp.int32, (CHUNK, CHUNK), 1)
    causal = r64 >= c64
    r128 = lax.broadcasted_iota(jnp.int32, (CHUNK, LANES), 0)
    l128 = lax.broadcasted_iota(jnp.int32, (CHUNK, LANES), 1)
    causal2 = r128 >= (l128 & (CHUNK - 1))
    first_half = lax.broadcasted_iota(jnp.int32, (1, LANES), 1) < CHUNK
    rb = lax.broadcasted_iota(jnp.int32, (LANES, LANES), 0)
    lb = lax.broadcasted_iota(jnp.int32, (LANES, LANES), 1)
    blockdiag = (rb >= CHUNK) == (lb >= CHUNK)
    b_off = D_SSD_P
    c_off = D_SSD_P + SSD_GROUPS * SSD_STATE
    groups = range(SSD_GROUPS)

    def chunk(c, carry):
        rows = pl.ds(pl.multiple_of(c * CHUNK, CHUNK), CHUNK)
        gl = [slice(g * SSD_GWP, (g + 1) * SSD_GWP) for g in groups]
        bm = [act_ref[rows, b_off + g * SSD_STATE:b_off + (g + 1) * SSD_STATE] for g in groups]
        cm = [act_ref[rows, c_off + g * SSD_STATE:c_off + (g + 1) * SSD_STATE] for g in groups]
        xdt = [xdt_ref[rows, gl[g]] for g in groups]
        cum_e = cume_ref[rows, :]
        last_e = cum_e[CHUNK - 1:CHUNK, :]
        cum_t = cumt_ref[c]

        cb2 = [_dot_nt(cm[g], jnp.concatenate([bm[g], bm[g]], axis=0)) for g in groups]
        y_diag = []
        for g in groups:
            hd = g * SSD_HPG
            cs01 = jnp.where(first_half, cum_t[hd:hd + 1, :], cum_t[hd + 1:hd + 2, :])
            ct01 = cum_e[:, g * SSD_GWP:g * SSD_GWP + LANES]
            seg01 = jnp.exp2(jnp.where(causal2, ct01 - cs01, -jnp.inf))
            ct2 = cum_e[:, g * SSD_GWP + LANES:g * SSD_GWP + LANES + CHUNK]
            seg2 = jnp.exp2(jnp.where(causal, ct2 - cum_t[hd + 2:hd + 3, 0:CHUNK], -jnp.inf))
            x01 = xdt[g][:, :LANES]
            xbd = jnp.where(blockdiag, jnp.concatenate([x01, x01], axis=0), 0.0)
            y01 = _dot(cb2[g] * seg01, xbd)
            y2 = _dot(cb2[g][:, :CHUNK] * seg2, xdt[g][:, LANES:])
            y_diag.append(jnp.concatenate([y01, y2], axis=1))

        y_off = []
        for g in groups:
            hst = state_ref[g]
            y_off.append(_dot(cm[g], hst) * jnp.exp2(cum_e[:, gl[g]]))
            dec = jnp.exp2(last_e[:, gl[g]] - cum_e[:, gl[g]])
            state_ref[g] = hst * jnp.exp2(last_e[:, gl[g]]) + _dot_tn(bm[g], xdt[g] * dec)

        for g in groups:
            y = y_diag[g] + y_off[g] + act_ref[rows, gl[g]] * dskip_ref[:, gl[g]]
            y = y * _silu(z_ref[rows, gl[g]])
            ms = jnp.sum(y * y, axis=-1, keepdims=True) * (1.0 / SSD_GW)
            y = y * lax.rsqrt(ms + EPS) * nw_ref[:, gl[g]]
            o_ref[0, rows, gl[g]] = y.astype(o_ref.dtype)
        return carry

    lax.fori_loop(0, n_chunks, chunk, 0)


def _ssd_call(u, w_ssd, conv_w, conv_b, dt_bias, alog_c, dskip_e, nw_e, t_blk):
    b, s, d = u.shape
    n_chunks = t_blk // CHUNK
    heads = jnp.arange(SSD_HEADS)
    lanes_e = (heads // SSD_HPG) * SSD_GWP + (heads % SSD_HPG) * SSD_HEAD_DIM
    col_e = jnp.arange(D_SSD_P)[None, :]
    expand = ((col_e >= lanes_e[:, None]) & (col_e < lanes_e[:, None] + SSD_HEAD_DIM))
    expand = jnp.pad(expand, ((0, LANES - SSD_HEADS), (0, 0))).astype(BF16)
    ti = jnp.arange(t_blk)
    btril = ((ti[:, None] >= ti[None, :]) & (ti[:, None] // CHUNK == ti[None, :] // CHUNK)).astype(BF16)
    cj = jnp.arange(2 * t_blk)
    dup = (ti[:, None] == (cj[None, :] // LANES) * CHUNK + cj[None, :] % CHUNK).astype(BF16)

    full = lambda shape: pl.BlockSpec(shape, lambda bi, si: (0,) * len(shape))
    return pl.pallas_call(
        functools.partial(_ssd_kernel, t_blk=t_blk),
        grid=(b, s // t_blk),
        in_specs=[pl.BlockSpec((1, t_blk, d), lambda bi, si: (bi, si, 0)),
                  full((d, SSD_COLS)),
                  full((SSD_CONV, SSD_XBC_P)), full((1, SSD_XBC_P)),
                  full((1, LANES)), full((1, LANES)),
                  full((1, D_SSD_P)), full((1, D_SSD_P)),
                  full((LANES, D_SSD_P)), full((t_blk, t_blk)), full((t_blk, 2 * t_blk))],
        out_specs=pl.BlockSpec((1, t_blk, D_SSD_P), lambda bi, si: (bi, si, 0)),
        out_shape=jax.ShapeDtypeStruct((b, s, D_SSD_P), BF16),
        scratch_shapes=[pltpu.VMEM((t_blk + CONV_HALO, SSD_XBC_P), F32),
                        pltpu.VMEM((t_blk, SSD_XBC_P), F32),
                        pltpu.VMEM((t_blk, D_SSD_P), F32),
                        pltpu.VMEM((t_blk, D_SSD_P), F32),
                        pltpu.VMEM((t_blk, D_SSD_P), F32),
                        pltpu.VMEM((n_chunks, LANES, LANES), F32),
                        pltpu.VMEM((SSD_GROUPS, SSD_STATE, SSD_GWP), F32)],
        compiler_params=pltpu.CompilerParams(
            dimension_semantics=("parallel", "arbitrary"),
            vmem_limit_bytes=VMEM_LIMIT),
        name="ssd_mixer",
    )(u, w_ssd, conv_w, conv_b, dt_bias, alog_c, dskip_e, nw_e, expand, btril, dup)


def _out_kernel(h_ref, ohg_ref, opool_ref, ossd_ref, p_ref, w1_ref, w2_ref, w3_ref, wpg_ref, wpe_ref,
                nw_ref, *out_refs, emit_h):
    dot = lambda a, w: jnp.dot(a, w, preferred_element_type=F32)
    acc = h_ref[...] + dot(ohg_ref[...], w1_ref[...]) + dot(opool_ref[...], w2_ref[...]) \
        + dot(ossd_ref[...], w3_ref[...])
    gate = _sigmoid(_dot(acc, wpg_ref[...]))
    hn = acc + gate * _dot(p_ref[...], wpe_ref[...])
    y = hn * lax.rsqrt(jnp.mean(hn * hn, axis=-1, keepdims=True) + EPS) * nw_ref[...]
    if emit_h:
        out_refs[0][...] = hn
        out_refs[1][...] = y.astype(out_refs[1].dtype)
    else:
        out_refs[0][...] = y


def _out_call(h2d, ohg, opool, ossd, p2d, w1, w2, w3, wpg, wpe, nw, tm, emit_h):
    n, d = h2d.shape
    rows = lambda width: pl.BlockSpec((tm, width), lambda i: (i, 0))
    full = lambda a: pl.BlockSpec(a.shape, lambda i: (0, 0))
    if emit_h:
        out_shape = (jax.ShapeDtypeStruct((n, d), F32), jax.ShapeDtypeStruct((n, d), BF16))
        out_specs = (rows(d), rows(d))
    else:
        out_shape = jax.ShapeDtypeStruct((n, d), F32)
        out_specs = rows(d)
    return pl.pallas_call(
        functools.partial(_out_kernel, emit_h=emit_h),
        grid=(n // tm,),
        in_specs=[rows(d), rows(D_HG), rows(D_POOL), rows(D_SSD_P), rows(P_DIM),
                  full(w1), full(w2), full(w3), full(wpg), full(wpe), pl.BlockSpec((1, d), lambda i: (0, 0))],
        out_specs=out_specs,
        out_shape=out_shape,
        compiler_params=pltpu.CompilerParams(
            dimension_semantics=("parallel",), vmem_limit_bytes=VMEM_LIMIT),
        name="out_proj",
    )(h2d, ohg, opool, ossd, p2d, w1, w2, w3, wpg, wpe, nw.reshape(1, d))


def _pad_groups(a):
    lead = a.shape[:-1]
    a = a.reshape(lead + (SSD_GROUPS, SSD_GW))
    a = jnp.pad(a, [(0, 0)] * len(lead) + [(0, 0), (0, SSD_GWP - SSD_GW)])
    return a.reshape(lead + (D_SSD_P,))


def _pad_lanes(a):
    return jnp.pad(a, [(0, 0)] * (a.ndim - 1) + [(0, LANES - a.shape[-1])])


C_POOL = 4 * D_HG
C_XBC = C_POOL + 2 * D_POOL
C_B = C_XBC + D_SSD
C_C = C_B + SSD_GROUPS * SSD_STATE
C_DT = C_C + SSD_GROUPS * SSD_STATE
C_Z = C_DT + SSD_HEADS
N_IN = C_Z + D_SSD
N_IN_MAIN = (N_IN // LANES) * LANES
WPREP_ROWS = 256


def _wprep_kernel(w_ref, tail_ref, hg_ref, pool_ref, ssd_ref):
    rows = w_ref.shape[1]
    lane = lax.broadcasted_iota(jnp.int32, (rows, LANES), 1)
    tail = jnp.where(lane < N_IN - N_IN_MAIN, tail_ref[0], 0.0)
    hg_ref[0] = w_ref[0, :, 0:C_POOL].astype(BF16)
    pool_ref[0] = w_ref[0, :, C_POOL:C_XBC].astype(BF16)

    def window(a0, width):
        if a0 + width <= N_IN_MAIN:
            return w_ref[0, :, a0:a0 + width]
        assert a0 + width == N_IN_MAIN + LANES
        return jnp.concatenate([w_ref[0, :, a0:N_IN_MAIN], tail], axis=1)

    lane_g = lax.broadcasted_iota(jnp.int32, (rows, SSD_GWP), 1)

    def group(a):
        a0 = (a // LANES) * LANES
        win = window(a0, SSD_GWP + LANES)
        if a != a0:
            win = pltpu.roll(win, SSD_GWP + LANES - (a - a0), 1)
        return jnp.where(lane_g < SSD_GW, win[:, :SSD_GWP], 0.0)

    for g in range(SSD_GROUPS):
        ssd_ref[0, :, g * SSD_GWP:(g + 1) * SSD_GWP] = group(C_XBC + g * SSD_GW).astype(BF16)
        zo = SSD_XBC_P + g * SSD_GWP
        ssd_ref[0, :, zo:zo + SSD_GWP] = group(C_Z + g * SSD_GW).astype(BF16)
    ssd_ref[0, :, D_SSD_P:SSD_XBC_P] = w_ref[0, :, C_B:C_DT].astype(BF16)
    dt = jnp.where(lane < SSD_HEADS, window(C_DT, LANES), 0.0)
    ssd_ref[0, :, SSD_XBC_P + D_SSD_P:] = dt.astype(BF16)


def _wprep_call(w_in):
    depth, d, n_in = w_in.shape
    assert n_in == N_IN and C_DT % LANES == 0 and d % WPREP_ROWS == 0
    out = lambda width: pl.BlockSpec((1, WPREP_ROWS, width), lambda i, r: (i, r, 0))
    return pl.pallas_call(
        _wprep_kernel,
        grid=(depth, d // WPREP_ROWS),
        in_specs=[pl.BlockSpec((1, WPREP_ROWS, N_IN_MAIN), lambda i, r: (i, r, 0)),
                  pl.BlockSpec((1, WPREP_ROWS, LANES), lambda i, r: (i, r, N_IN_MAIN // LANES))],
        out_specs=(out(C_POOL), out(C_XBC - C_POOL), out(SSD_COLS)),
        out_shape=(jax.ShapeDtypeStruct((depth, d, C_POOL), BF16),
                   jax.ShapeDtypeStruct((depth, d, C_XBC - C_POOL), BF16),
                   jax.ShapeDtypeStruct((depth, d, SSD_COLS), BF16)),
        compiler_params=pltpu.CompilerParams(
            dimension_semantics=("parallel", "parallel"), vmem_limit_bytes=VMEM_LIMIT),
        name="weight_prep",
    )(w_in, w_in)


def kernel(x, p, norm_w, w_in, hg_lb, hg_norm_w, pool_w, pool_scale, conv_w, conv_b, dt_bias, a_log, d_skip,
           ssd_norm_w, w_out, w_pe, w_pg, final_norm_w):
    b, s, d = x.shape
    depth = w_in.shape[0]
    n = b * s
    t_blk = min(256, s)
    t_hg = min(512, s)
    tm = min(512, n)

    lb_all = jnp.cumsum(jax.nn.softmax(hg_lb.astype(F32), axis=0), axis=0)
    lb_all = lb_all - lb_all[0]

    o_hg0, o_pool0 = 0, D_HG
    o_ssd0 = D_HG + D_POOL

    w_hg_all, w_pool_all, w_ssd_all = _wprep_call(w_in)

    h = x.reshape(n, d)
    u = _rms_norm_call(h, norm_w[0], tm).reshape(b, s, d)
    out = None
    for i in range(depth):
        lb = lb_all[i].reshape(1, D_HG)
        o_hg = _hgrn_call(u, w_hg_all[i], jnp.log(lb), jnp.log1p(-lb),
                          hg_norm_w[i].reshape(1, D_HG), t_hg)
        o_pool = _pool_call(u, w_pool_all[i], pool_w[i].astype(BF16),
                            pool_scale[i].reshape(1, D_POOL), t_blk)
        xbc_split = lambda a: jnp.concatenate([_pad_groups(a[..., :D_SSD]), a[..., D_SSD:]], axis=-1)
        rep = lambda a: _pad_groups(jnp.repeat(a, SSD_HEAD_DIM)).reshape(1, D_SSD_P)
        o_ssd = _ssd_call(u, w_ssd_all[i], xbc_split(conv_w[i]), xbc_split(conv_b[i]).reshape(1, SSD_XBC_P),
                          _pad_lanes(dt_bias[i]).reshape(1, LANES), _pad_lanes(a_log[i]).reshape(1, LANES),
                          rep(d_skip[i]), _pad_groups(ssd_norm_w[i]).reshape(1, D_SSD_P), t_blk)
        wo = w_out[i]
        w3 = jnp.pad(wo[o_ssd0:].reshape(SSD_GROUPS, SSD_GW, d), ((0, 0), (0, SSD_GWP - SSD_GW), (0, 0)))
        last = i == depth - 1
        nw_next = final_norm_w if last else norm_w[i + 1]
        res = _out_call(h, o_hg.reshape(n, D_HG), o_pool.reshape(n, D_POOL), o_ssd.reshape(n, D_SSD_P),
                        p[i].reshape(n, P_DIM), wo[o_hg0:o_pool0].astype(BF16), wo[o_pool0:o_ssd0].astype(BF16),
                        w3.reshape(D_SSD_P, d).astype(BF16), w_pg[i].astype(BF16), w_pe[i].astype(BF16),
                        nw_next, tm, emit_h=not last)
        if last:
            out = res
        else:
            h, u = res
            u = u.reshape(b, s, d)
    return out.reshape(b, s, d)
```

```python
import functools

import jax
import jax.numpy as jnp
from jax import lax
from jax.experimental import pallas as pl
from jax.experimental.pallas import tpu as pltpu

F32 = jnp.float32
BF16 = jnp.bfloat16

CHUNK = 64
EPS = 1e-6
D_MODEL = 1024
P_DIM = 256
HG_HEADS = 6
HG_D = 128
HG_PAIR = 2
HG_LEVELS = (64, 32, 16, 8, 4)
HG_UNROLL = 4
LOG2E = 1.4426950408889634
D_HG = HG_HEADS * HG_D
POOL_WINDOWS = (2, 4, 8, 16)
POOL_CH = 128
D_POOL = len(POOL_WINDOWS) * POOL_CH
POOL_HALO = 16
SSD_HEADS = 12
SSD_HEAD_DIM = 64
SSD_GROUPS = 4
SSD_HPG = SSD_HEADS // SSD_GROUPS
SSD_GW = SSD_HPG * SSD_HEAD_DIM
SSD_GWP = 256
D_SSD = SSD_HEADS * SSD_HEAD_DIM
D_SSD_P = SSD_GROUPS * SSD_GWP
SSD_STATE = 128
SSD_CONV = 4
CONV_HALO = 8
SSD_UNROLL = 2
LANES = 128
SSD_XBC_P = D_SSD_P + 2 * SSD_GROUPS * SSD_STATE
SSD_COLS = SSD_XBC_P + D_SSD_P + LANES
VMEM_LIMIT = 48 * 1024 * 1024


def _dot(a, b):
    return jnp.dot(a.astype(BF16), b.astype(BF16), preferred_element_type=F32)


def _dot_nt(a, b):
    return lax.dot_general(a.astype(BF16), b.astype(BF16), (((1,), (1,)), ((), ())),
                           preferred_element_type=F32)


def _dot_tn(a, b):
    return lax.dot_general(a.astype(BF16), b.astype(BF16), (((0,), (0,)), ((), ())),
                           preferred_element_type=F32)


def _split3(x):
    hi = x.astype(BF16)
    r1 = x - hi.astype(F32)
    mid = r1.astype(BF16)
    lo = (r1 - mid.astype(F32)).astype(BF16)
    return hi, mid, lo


def _sel_left(m, x):
    hi, mid, lo = _split3(x)
    d = lambda p: jnp.dot(m, p, preferred_element_type=F32)
    return d(hi) + (d(mid) + d(lo))


def _sel_right(x, m):
    hi, mid, lo = _split3(x)
    d = lambda p: jnp.dot(p, m, preferred_element_type=F32)
    return d(hi) + (d(mid) + d(lo))


def _sel_left3(m3, x):
    return jnp.dot(m3, jnp.concatenate(_split3(x), axis=0), preferred_element_type=F32)


def _sel_right3(x, m3):
    return jnp.dot(jnp.concatenate(_split3(x), axis=1), m3, preferred_element_type=F32)


def _sigmoid(x):
    return 1.0 / (1.0 + jnp.exp(-x))


def _silu(x):
    return x * _sigmoid(x)


def _softplus(x):
    return jnp.maximum(x, 0.0) + jnp.log1p(jnp.exp(-jnp.abs(x)))


def _tril_bf16(n):
    r = lax.broadcasted_iota(jnp.int32, (n, n), 0)
    c = lax.broadcasted_iota(jnp.int32, (n, n), 1)
    return (r >= c).astype(BF16)


def _norm_kernel(x_ref, w_ref, o_ref):
    x = x_ref[...]
    y = x * lax.rsqrt(jnp.mean(x * x, axis=-1, keepdims=True) + EPS)
    o_ref[...] = (y * w_ref[...]).astype(o_ref.dtype)


def _rms_norm_call(x2d, w, tm):
    n, d = x2d.shape
    return pl.pallas_call(
        _norm_kernel,
        grid=(n // tm,),
        in_specs=[pl.BlockSpec((tm, d), lambda i: (i, 0)),
                  pl.BlockSpec((1, d), lambda i: (0, 0))],
        out_specs=pl.BlockSpec((tm, d), lambda i: (i, 0)),
        out_shape=jax.ShapeDtypeStruct((n, d), BF16),
        compiler_params=pltpu.CompilerParams(dimension_semantics=("parallel",)),
        name="rms_norm_in",
    )(x2d, w.reshape(1, d))


def _hgrn_kernel(u_ref, wq_ref, wf_ref, wv_ref, wg_ref, loglb_ref, log1mlb_ref, nw_ref, o_ref,
                 proj_ref, cum_ref, pair_ref, state_ref, *, n_chunks):
    @pl.when(pl.program_id(2) == 0)
    def _():
        state_ref[...] = jnp.zeros_like(state_ref)

    pw = HG_PAIR * HG_D
    u = u_ref[0]
    for kind, w_ref in enumerate((wq_ref, wf_ref, wv_ref, wg_ref)):
        proj_ref[:, kind * pw:(kind + 1) * pw] = jnp.dot(u, w_ref[...], preferred_element_type=F32)

    row = lax.broadcasted_iota(jnp.int32, (CHUNK, HG_D), 0)
    sub = lax.broadcasted_iota(jnp.int32, (CHUNK // 8, 8, HG_D), 1)
    r64 = lax.broadcasted_iota(jnp.int32, (CHUNK, CHUNK), 0)
    c64 = lax.broadcasted_iota(jnp.int32, (CHUNK, CHUNK), 1)
    r3 = lax.broadcasted_iota(jnp.int32, (CHUNK, 3 * CHUNK), 0)
    c3 = lax.broadcasted_iota(jnp.int32, (CHUNK, 3 * CHUNK), 1)
    tril3 = (r3 >= (c3 % CHUNK)).astype(BF16)
    level_mask = {bs: ((r64 // bs) == (c64 // bs)) & ((r64 & (bs // 2)) != 0) & ((c64 & (bs // 2)) == 0)
                  for bs in HG_LEVELS}
    level_sign = {bs: jnp.where((row & (bs // 2)) != 0, 1.0, -1.0) for bs in HG_LEVELS}
    diag0 = r64 == c64
    diag1 = (r64 == c64 + 1) & ((r64 & 1) == 1)
    odd = (row & 1) == 1

    def ref_rows(cum, bs):
        half = bs // 2
        if bs >= 8:
            c3 = cum.reshape(CHUNK // bs, bs, HG_D)
            return jnp.broadcast_to(c3[:, half - 1:half, :], c3.shape).reshape(CHUNK, HG_D)
        c3 = cum.reshape(CHUNK // 8, 8, HG_D)
        lo = jnp.broadcast_to(c3[:, 1:2, :], c3.shape)
        hi = jnp.broadcast_to(c3[:, 5:6, :], c3.shape)
        return jnp.where(sub < 4, lo, hi).reshape(CHUNK, HG_D)

    def gates(c, carry):
        rows = pl.ds(pl.multiple_of(c * CHUNK, CHUNK), CHUNK)
        for hh in range(HG_PAIR):
            lanes = slice(hh * HG_D, (hh + 1) * HG_D)
            z = proj_ref[rows, pw + hh * HG_D:pw + (hh + 1) * HG_D]
            loglb = loglb_ref[:, lanes]
            log1mlb = log1mlb_ref[:, lanes]
            zs = z * LOG2E
            log_sig = jnp.minimum(zs, 0.0) - jnp.log2(1.0 + jnp.exp2(-jnp.abs(zs)))
            b_ = log1mlb + log_sig
            log_f = jnp.maximum(loglb, b_) + jnp.log2(1.0 + jnp.exp2(-jnp.abs(loglb - b_)))
            f = jnp.exp2(log_f)
            kk = 1.0 - f
            proj_ref[rows, pw + hh * HG_D:pw + (hh + 1) * HG_D] = kk
            cum_ref[rows, lanes] = _sel_left3(tril3, log_f)
            pair_ref[rows, lanes] = jnp.where(odd, pltpu.roll(kk, 1, 0) * f, 0.0)
        return carry

    lax.fori_loop(0, n_chunks, gates, 0, unroll=HG_UNROLL)

    def trip(c, carry):
        chains = [(j, hh) for j in range(HG_UNROLL) for hh in range(HG_PAIR)]
        rows = [pl.ds(pl.multiple_of((c * HG_UNROLL + j) * CHUNK, CHUNK), CHUNK) for j in range(HG_UNROLL)]
        lanes = [slice(hh * HG_D, (hh + 1) * HG_D) for hh in range(HG_PAIR)]
        col = lambda kind, hh: slice(kind * pw + hh * HG_D, kind * pw + (hh + 1) * HG_D)
        q = {ch: proj_ref[rows[ch[0]], col(0, ch[1])] for ch in chains}
        kk = {ch: proj_ref[rows[ch[0]], col(1, ch[1])] for ch in chains}
        cum = {ch: cum_ref[rows[ch[0]], lanes[ch[1]]] for ch in chains}

        scores = {}
        for ch in chains:
            acc = None
            for bs in HG_LEVELS:
                e = jnp.exp2((cum[ch] - ref_rows(cum[ch], bs)) * level_sign[bs])
                m = jnp.where(level_mask[bs], _dot_nt(q[ch] * e, kk[ch] * e), 0.0)
                acc = m if acc is None else acc + m
            rs0 = jnp.sum(q[ch] * kk[ch], axis=1, keepdims=True)
            rs1 = jnp.sum(q[ch] * pair_ref[rows[ch[0]], lanes[ch[1]]], axis=1, keepdims=True)
            scores[ch] = acc + jnp.where(diag0, rs0, 0.0) + jnp.where(diag1, rs1, 0.0)

        inter = {}
        for ch in chains:
            j, hh = ch
            last = cum[ch][CHUNK - 1:CHUNK, :]
            st = state_ref[hh]
            inter[ch] = _dot_nt(q[ch] * jnp.exp2(cum[ch]), st)
            v = proj_ref[rows[j], col(2, hh)]
            state_ref[hh] = st * jnp.exp2(last) + _dot_tn(v, kk[ch] * jnp.exp2(last - cum[ch]))

        for ch in chains:
            j, hh = ch
            o = inter[ch] + _dot(scores[ch], proj_ref[rows[j], col(2, hh)])
            o = o * lax.rsqrt(jnp.mean(o * o, axis=-1, keepdims=True) + EPS) * nw_ref[:, lanes[hh]]
            g = proj_ref[rows[j], col(3, hh)]
            o_ref[0, rows[j], lanes[hh]] = (o * _silu(g)).astype(o_ref.dtype)
        return carry

    lax.fori_loop(0, n_chunks // HG_UNROLL, trip, 0)


def _hgrn_call(u, w_hg, loglb, log1mlb, nw, t_blk):
    b, s, d = u.shape
    n_pairs = HG_HEADS // HG_PAIR
    pw = HG_PAIR * HG_D
    grid = (b, n_pairs, s // t_blk)
    vec = pl.BlockSpec((1, pw), lambda bi, hp, si: (0, hp))
    wspec = lambda kind: pl.BlockSpec((d, pw), lambda bi, hp, si: (0, kind * n_pairs + hp))
    return pl.pallas_call(
        functools.partial(_hgrn_kernel, n_chunks=t_blk // CHUNK),
        grid=grid,
        in_specs=[pl.BlockSpec((1, t_blk, d), lambda bi, hp, si: (bi, si, 0)),
                  wspec(0), wspec(1), wspec(2), wspec(3),
                  vec, vec, vec],
        out_specs=pl.BlockSpec((1, t_blk, pw), lambda bi, hp, si: (bi, si, hp)),
        out_shape=jax.ShapeDtypeStruct((b, s, D_HG), BF16),
        scratch_shapes=[pltpu.VMEM((t_blk, 4 * pw), F32),
                        pltpu.VMEM((t_blk, pw), F32),
                        pltpu.VMEM((t_blk, pw), F32),
                        pltpu.VMEM((HG_PAIR, HG_D, HG_D), F32)],
        compiler_params=pltpu.CompilerParams(
            dimension_semantics=("parallel", "parallel", "arbitrary"),
            vmem_limit_bytes=VMEM_LIMIT),
        name="hgrn2_mixer",
    )(u, w_hg, w_hg, w_hg, w_hg, loglb, log1mlb, nw)


def _pool_kernel(u_ref, w_ref, pw_ref, ps_ref, o_ref, ext_ref, *, t_blk):
    si = pl.program_id(1)

    @pl.when(si == 0)
    def _():
        ext_ref[0:POOL_HALO, :] = jnp.zeros((POOL_HALO, D_POOL), F32)

    @pl.when(si > 0)
    def _():
        ext_ref[0:POOL_HALO, :] = ext_ref[t_blk:t_blk + POOL_HALO, :]

    proj = jnp.dot(u_ref[0], w_ref[...], preferred_element_type=F32)
    ext_ref[POOL_HALO:POOL_HALO + t_blk, :] = proj[:, :D_POOL]

    pos = (si * t_blk + 1 + lax.broadcasted_iota(jnp.int32, (t_blk, POOL_CH), 0)).astype(F32)
    for gi, win in enumerate(POOL_WINDOWS):
        lo = gi * POOL_CH
        acc = ext_ref[:, lo:lo + POOL_CH]
        shift = 1
        while shift < win:
            acc = acc + pltpu.roll(acc, shift, 0)
            shift *= 2
        cur = proj[:, lo:lo + POOL_CH]
        pooled = acc[POOL_HALO:, :] / jnp.minimum(pos, float(win)) - cur
        y = _dot(pooled, pw_ref[gi]) * ps_ref[:, lo:lo + POOL_CH]
        gate = proj[:, D_POOL + lo:D_POOL + lo + POOL_CH]
        o_ref[0, :, lo:lo + POOL_CH] = (y * _silu(gate)).astype(o_ref.dtype)


def _pool_call(u, w_pool, pool_w, pool_scale, t_blk):
    b, s, d = u.shape
    return pl.pallas_call(
        functools.partial(_pool_kernel, t_blk=t_blk),
        grid=(b, s // t_blk),
        in_specs=[pl.BlockSpec((1, t_blk, d), lambda bi, si: (bi, si, 0)),
                  pl.BlockSpec((d, 2 * D_POOL), lambda bi, si: (0, 0)),
                  pl.BlockSpec((len(POOL_WINDOWS), POOL_CH, POOL_CH), lambda bi, si: (0, 0, 0)),
                  pl.BlockSpec((1, D_POOL), lambda bi, si: (0, 0))],
        out_specs=pl.BlockSpec((1, t_blk, D_POOL), lambda bi, si: (bi, si, 0)),
        out_shape=jax.ShapeDtypeStruct((b, s, D_POOL), BF16),
        scratch_shapes=[pltpu.VMEM((t_blk + POOL_HALO, D_POOL), F32)],
        compiler_params=pltpu.CompilerParams(
            dimension_semantics=("parallel", "arbitrary"),
            vmem_limit_bytes=VMEM_LIMIT),
        name="pool_mixer",
    )(u, w_pool, pool_w, pool_scale)


def _ssd_kernel(u_ref, w_ref, cw_ref, cb_ref, dtb_ref, alogc_ref, dskip_ref, nw_ref,
                exp_ref, btril_ref, dup_ref, o_ref,
                ext_ref, act_ref, z_ref, xdt_ref, cume_ref, cumt_ref, state_ref, *, t_blk):
    si = pl.program_id(1)
    n_chunks = t_blk // CHUNK

    @pl.when(si == 0)
    def _():
        ext_ref[0:CONV_HALO, :] = jnp.zeros((CONV_HALO, SSD_XBC_P), F32)
        state_ref[...] = jnp.zeros_like(state_ref)

    @pl.when(si > 0)
    def _():
        ext_ref[0:CONV_HALO, :] = ext_ref[t_blk:t_blk + CONV_HALO, :]

    proj = jnp.dot(u_ref[0], w_ref[...], preferred_element_type=F32)
    ext_ref[CONV_HALO:CONV_HALO + t_blk, :] = proj[:, :SSD_XBC_P]
    z_ref[...] = proj[:, SSD_XBC_P:SSD_XBC_P + D_SSD_P]

    e = ext_ref[...]
    xc = cw_ref[3:4, :] * e[CONV_HALO:, :] + cb_ref[...]
    for j in range(1, SSD_CONV):
        xc = xc + cw_ref[3 - j:4 - j, :] * pltpu.roll(e, j, 0)[CONV_HALO:, :]
    xc = _silu(xc)
    act_ref[...] = xc

    dt = _softplus(proj[:, SSD_XBC_P + D_SSD_P:] + dtb_ref[...])
    exp3 = jnp.concatenate([exp_ref[...]] * 3, axis=0)
    xdt_ref[...] = xc[:, :D_SSD_P] * _sel_right3(dt, exp3)

    cum_c = _sel_left(btril_ref[...], dt * (-jnp.exp(alogc_ref[...]) * LOG2E))
    hi, mid, lo = _split3(cum_c)
    cume_ref[...] = jnp.dot(jnp.concatenate([hi, mid, lo], axis=1), exp3,
                            preferred_element_type=F32)
    cum_t = lax.dot_general(jnp.concatenate([hi, mid, lo], axis=0), jnp.concatenate([dup_ref[...]] * 3, axis=0),
                            (((0,), (0,)), ((), ())), preferred_element_type=F32)
    for c in range(n_chunks):
        cumt_ref[c] = cum_t[:, c * LANES:(c + 1) * LANES]

    r64 = lax.broadcasted_iota(jnp.int32, (CHUNK, CHUNK), 0)
    c64 = lax.broadcasted_iota(jnp.int32, (CHUNK, CHUNK), 1)
    causal = r64 >= c64
    r128 = lax.broadcasted_iota(jnp.int32, (CHUNK, LANES), 0)
    l128 = lax.broadcasted_iota(jnp.int32, (CHUNK, LANES), 1)
    causal2 = r128 >= (l128 & (CHUNK - 1))
    first_half = lax.broadcasted_iota(jnp.int32, (1, LANES), 1) < CHUNK
    rb = lax.broadcasted_iota(jnp.int32, (LANES, LANES), 0)
    lb = lax.broadcasted_iota(jnp.int32, (LANES, LANES), 1)
    blockdiag = (rb >= CHUNK) == (lb >= CHUNK)
    b_off = D_SSD_P
    c_off = D_SSD_P + SSD_GROUPS * SSD_STATE
    groups = range(SSD_GROUPS)

    gl = [slice(g * SSD_GWP, (g + 1) * SSD_GWP) for g in groups]
    bl = [slice(b_off + g * SSD_STATE, b_off + (g + 1) * SSD_STATE) for g in groups]
    cl = [slice(c_off + g * SSD_STATE, c_off + (g + 1) * SSD_STATE) for g in groups]

    def trip(c, carry):
        chains = [(j, g) for j in range(SSD_UNROLL) for g in groups]
        rows = [pl.ds(pl.multiple_of((c * SSD_UNROLL + j) * CHUNK, CHUNK), CHUNK) for j in range(SSD_UNROLL)]
        cum_t = [cumt_ref[c * SSD_UNROLL + j] for j in range(SSD_UNROLL)]

        cb2 = {}
        for j, g in chains:
            bm = act_ref[rows[j], bl[g]]
            cb2[j, g] = _dot_nt(act_ref[rows[j], cl[g]], jnp.concatenate([bm, bm], axis=0))
        y_diag = {}
        for j, g in chains:
            hd = g * SSD_HPG
            cs01 = jnp.where(first_half, cum_t[j][hd:hd + 1, :], cum_t[j][hd + 1:hd + 2, :])
            ct01 = cume_ref[rows[j], g * SSD_GWP:g * SSD_GWP + LANES]
            seg01 = jnp.exp2(jnp.where(causal2, ct01 - cs01, -jnp.inf))
            ct2 = cume_ref[rows[j], g * SSD_GWP + LANES:g * SSD_GWP + LANES + CHUNK]
            seg2 = jnp.exp2(jnp.where(causal, ct2 - cum_t[j][hd + 2:hd + 3, 0:CHUNK], -jnp.inf))
            x01 = xdt_ref[rows[j], g * SSD_GWP:g * SSD_GWP + LANES]
            xbd = jnp.where(blockdiag, jnp.concatenate([x01, x01], axis=0), 0.0)
            y01 = _dot(cb2[j, g] * seg01, xbd)
            y2 = _dot(cb2[j, g][:, :CHUNK] * seg2, xdt_ref[rows[j], g * SSD_GWP + LANES:(g + 1) * SSD_GWP])
            y_diag[j, g] = jnp.concatenate([y01, y2], axis=1)

        y_off = {}
        for j, g in chains:
            cum_e = cume_ref[rows[j], gl[g]]
            last_e = cum_e[CHUNK - 1:CHUNK, :]
            hst = state_ref[g]
            y_off[j, g] = _dot(act_ref[rows[j], cl[g]], hst) * jnp.exp2(cum_e)
            xdec = xdt_ref[rows[j], gl[g]] * jnp.exp2(last_e - cum_e)
            state_ref[g] = hst * jnp.exp2(last_e) + _dot_tn(act_ref[rows[j], bl[g]], xdec)

        for j, g in chains:
            y = y_diag[j, g] + y_off[j, g] + act_ref[rows[j], gl[g]] * dskip_ref[:, gl[g]]
            y = y * _silu(z_ref[rows[j], gl[g]])
            ms = jnp.sum(y * y, axis=-1, keepdims=True) * (1.0 / SSD_GW)
            y = y * lax.rsqrt(ms + EPS) * nw_ref[:, gl[g]]
            o_ref[0, rows[j], gl[g]] = y.astype(o_ref.dtype)
        return carry

    lax.fori_loop(0, n_chunks // SSD_UNROLL, trip, 0)


def _ssd_call(u, w_ssd, conv_w, conv_b, dt_bias, alog_c, dskip_e, nw_e, t_blk):
    b, s, d = u.shape
    n_chunks = t_blk // CHUNK
    heads = jnp.arange(SSD_HEADS)
    lanes_e = (heads // SSD_HPG) * SSD_GWP + (heads % SSD_HPG) * SSD_HEAD_DIM
    col_e = jnp.arange(D_SSD_P)[None, :]
    expand = ((col_e >= lanes_e[:, None]) & (col_e < lanes_e[:, None] + SSD_HEAD_DIM))
    expand = jnp.pad(expand, ((0, LANES - SSD_HEADS), (0, 0))).astype(BF16)
    ti = jnp.arange(t_blk)
    btril = ((ti[:, None] >= ti[None, :]) & (ti[:, None] // CHUNK == ti[None, :] // CHUNK)).astype(BF16)
    cj = jnp.arange(2 * t_blk)
    dup = (ti[:, None] == (cj[None, :] // LANES) * CHUNK + cj[None, :] % CHUNK).astype(BF16)

    full = lambda shape: pl.BlockSpec(shape, lambda bi, si: (0,) * len(shape))
    return pl.pallas_call(
        functools.partial(_ssd_kernel, t_blk=t_blk),
        grid=(b, s // t_blk),
        in_specs=[pl.BlockSpec((1, t_blk, d), lambda bi, si: (bi, si, 0)),
                  full((d, SSD_COLS)),
                  full((SSD_CONV, SSD_XBC_P)), full((1, SSD_XBC_P)),
                  full((1, LANES)), full((1, LANES)),
                  full((1, D_SSD_P)), full((1, D_SSD_P)),
                  full((LANES, D_SSD_P)), full((t_blk, t_blk)), full((t_blk, 2 * t_blk))],
        out_specs=pl.BlockSpec((1, t_blk, D_SSD_P), lambda bi, si: (bi, si, 0)),
        out_shape=jax.ShapeDtypeStruct((b, s, D_SSD_P), BF16),
        scratch_shapes=[pltpu.VMEM((t_blk + CONV_HALO, SSD_XBC_P), F32),
                        pltpu.VMEM((t_blk, SSD_XBC_P), F32),
                        pltpu.VMEM((t_blk, D_SSD_P), F32),
                        pltpu.VMEM((t_blk, D_SSD_P), F32),
                        pltpu.VMEM((t_blk, D_SSD_P), F32),
                        pltpu.VMEM((n_chunks, LANES, LANES), F32),
                        pltpu.VMEM((SSD_GROUPS, SSD_STATE, SSD_GWP), F32)],
        compiler_params=pltpu.CompilerParams(
            dimension_semantics=("parallel", "arbitrary"),
            vmem_limit_bytes=VMEM_LIMIT),
        name="ssd_mixer",
    )(u, w_ssd, conv_w, conv_b, dt_bias, alog_c, dskip_e, nw_e, expand, btril, dup)


def _out_kernel(h_ref, ohg_ref, opool_ref, ossd_ref, p_ref, w1_ref, w2_ref, w3_ref, wpg_ref, wpe_ref,
                nw_ref, *out_refs, emit_h):
    dot = lambda a, w: jnp.dot(a, w, preferred_element_type=F32)
    acc = h_ref[...] + dot(ohg_ref[...], w1_ref[...]) + dot(opool_ref[...], w2_ref[...]) \
        + dot(ossd_ref[...], w3_ref[...])
    gate = _sigmoid(_dot(acc, wpg_ref[...]))
    hn = acc + gate * _dot(p_ref[...], wpe_ref[...])
    y = hn * lax.rsqrt(jnp.mean(hn * hn, axis=-1, keepdims=True) + EPS) * nw_ref[...]
    if emit_h:
        out_refs[0][...] = hn
        out_refs[1][...] = y.astype(out_refs[1].dtype)
    else:
        out_refs[0][...] = y


def _out_call(h2d, ohg, opool, ossd, p2d, w1, w2, w3, wpg, wpe, nw, tm, emit_h):
    n, d = h2d.shape
    rows = lambda width: pl.BlockSpec((tm, width), lambda i: (i, 0))
    full = lambda a: pl.BlockSpec(a.shape, lambda i: (0, 0))
    if emit_h:
        out_shape = (jax.ShapeDtypeStruct((n, d), F32), jax.ShapeDtypeStruct((n, d), BF16))
        out_specs = (rows(d), rows(d))
    else:
        out_shape = jax.ShapeDtypeStruct((n, d), F32)
        out_specs = rows(d)
    return pl.pallas_call(
        functools.partial(_out_kernel, emit_h=emit_h),
        grid=(n // tm,),
        in_specs=[rows(d), rows(D_HG), rows(D_POOL), rows(D_SSD_P), rows(P_DIM),
                  full(w1), full(w2), full(w3), full(wpg), full(wpe), pl.BlockSpec((1, d), lambda i: (0, 0))],
        out_specs=out_specs,
        out_shape=out_shape,
        compiler_params=pltpu.CompilerParams(
            dimension_semantics=("parallel",), vmem_limit_bytes=VMEM_LIMIT),
        name="out_proj",
    )(h2d, ohg, opool, ossd, p2d, w1, w2, w3, wpg, wpe, nw.reshape(1, d))


def _pad_groups(a):
    lead = a.shape[:-1]
    a = a.reshape(lead + (SSD_GROUPS, SSD_GW))
    a = jnp.pad(a, [(0, 0)] * len(lead) + [(0, 0), (0, SSD_GWP - SSD_GW)])
    return a.reshape(lead + (D_SSD_P,))


def _pad_lanes(a):
    return jnp.pad(a, [(0, 0)] * (a.ndim - 1) + [(0, LANES - a.shape[-1])])


C_POOL = 4 * D_HG
C_XBC = C_POOL + 2 * D_POOL
C_B = C_XBC + D_SSD
C_C = C_B + SSD_GROUPS * SSD_STATE
C_DT = C_C + SSD_GROUPS * SSD_STATE
C_Z = C_DT + SSD_HEADS
N_IN = C_Z + D_SSD
N_IN_MAIN = (N_IN // LANES) * LANES
WPREP_ROWS = 256


def _wprep_kernel(w_ref, tail_ref, hg_ref, pool_ref, ssd_ref):
    rows = w_ref.shape[1]
    lane = lax.broadcasted_iota(jnp.int32, (rows, LANES), 1)
    tail = jnp.where(lane < N_IN - N_IN_MAIN, tail_ref[0], 0.0)
    hg_ref[0] = w_ref[0, :, 0:C_POOL].astype(BF16)
    pool_ref[0] = w_ref[0, :, C_POOL:C_XBC].astype(BF16)

    def window(a0, width):
        if a0 + width <= N_IN_MAIN:
            return w_ref[0, :, a0:a0 + width]
        assert a0 + width == N_IN_MAIN + LANES
        return jnp.concatenate([w_ref[0, :, a0:N_IN_MAIN], tail], axis=1)

    lane_g = lax.broadcasted_iota(jnp.int32, (rows, SSD_GWP), 1)

    def group(a):
        a0 = (a // LANES) * LANES
        win = window(a0, SSD_GWP + LANES)
        if a != a0:
            win = pltpu.roll(win, SSD_GWP + LANES - (a - a0), 1)
        return jnp.where(lane_g < SSD_GW, win[:, :SSD_GWP], 0.0)

    for g in range(SSD_GROUPS):
        ssd_ref[0, :, g * SSD_GWP:(g + 1) * SSD_GWP] = group(C_XBC + g * SSD_GW).astype(BF16)
        zo = SSD_XBC_P + g * SSD_GWP
        ssd_ref[0, :, zo:zo + SSD_GWP] = group(C_Z + g * SSD_GW).astype(BF16)
    ssd_ref[0, :, D_SSD_P:SSD_XBC_P] = w_ref[0, :, C_B:C_DT].astype(BF16)
    dt = jnp.where(lane < SSD_HEADS, window(C_DT, LANES), 0.0)
    ssd_ref[0, :, SSD_XBC_P + D_SSD_P:] = dt.astype(BF16)


def _wprep_call(w_in):
    depth, d, n_in = w_in.shape
    assert n_in == N_IN and C_DT % LANES == 0 and d % WPREP_ROWS == 0
    out = lambda width: pl.BlockSpec((1, WPREP_ROWS, width), lambda i, r: (i, r, 0))
    return pl.pallas_call(
        _wprep_kernel,
        grid=(depth, d // WPREP_ROWS),
        in_specs=[pl.BlockSpec((1, WPREP_ROWS, N_IN_MAIN), lambda i, r: (i, r, 0)),
                  pl.BlockSpec((1, WPREP_ROWS, LANES), lambda i, r: (i, r, N_IN_MAIN // LANES))],
        out_specs=(out(C_POOL), out(C_XBC - C_POOL), out(SSD_COLS)),
        out_shape=(jax.ShapeDtypeStruct((depth, d, C_POOL), BF16),
                   jax.ShapeDtypeStruct((depth, d, C_XBC - C_POOL), BF16),
                   jax.ShapeDtypeStruct((depth, d, SSD_COLS), BF16)),
        compiler_params=pltpu.CompilerParams(
            dimension_semantics=("parallel", "parallel"), vmem_limit_bytes=VMEM_LIMIT),
        name="weight_prep",
    )(w_in, w_in)


def kernel(x, p, norm_w, w_in, hg_lb, hg_norm_w, pool_w, pool_scale, conv_w, conv_b, dt_bias, a_log, d_skip,
           ssd_norm_w, w_out, w_pe, w_pg, final_norm_w):
    b, s, d = x.shape
    depth = w_in.shape[0]
    n = b * s
    t_blk = min(256, s)
    t_hg = min(1024, s)
    t_ssd = min(256, s)
    tm = min(512, n)

    lb_all = jnp.cumsum(jax.nn.softmax(hg_lb.astype(F32), axis=0), axis=0)
    lb_all = lb_all - lb_all[0]

    o_hg0, o_pool0 = 0, D_HG
    o_ssd0 = D_HG + D_POOL

    w_hg_all, w_pool_all, w_ssd_all = _wprep_call(w_in)

    h = x.reshape(n, d)
    u = _rms_norm_call(h, norm_w[0], tm).reshape(b, s, d)
    out = None
    for i in range(depth):
        lb = lb_all[i].reshape(1, D_HG)
        o_hg = _hgrn_call(u, w_hg_all[i], jnp.log(lb) * LOG2E, jnp.log1p(-lb) * LOG2E,
                          hg_norm_w[i].reshape(1, D_HG), t_hg)
        o_pool = _pool_call(u, w_pool_all[i], pool_w[i].astype(BF16),
                            pool_scale[i].reshape(1, D_POOL), t_hg)
        xbc_split = lambda a: jnp.concatenate([_pad_groups(a[..., :D_SSD]), a[..., D_SSD:]], axis=-1)
        rep = lambda a: _pad_groups(jnp.repeat(a, SSD_HEAD_DIM)).reshape(1, D_SSD_P)
        o_ssd = _ssd_call(u, w_ssd_all[i], xbc_split(conv_w[i]), xbc_split(conv_b[i]).reshape(1, SSD_XBC_P),
                          _pad_lanes(dt_bias[i]).reshape(1, LANES), _pad_lanes(a_log[i]).reshape(1, LANES),
                          rep(d_skip[i]), _pad_groups(ssd_norm_w[i]).reshape(1, D_SSD_P), t_ssd)
        wo = w_out[i]
        w3 = jnp.pad(wo[o_ssd0:].reshape(SSD_GROUPS, SSD_GW, d), ((0, 0), (0, SSD_GWP - SSD_GW), (0, 0)))
        last = i == depth - 1
        nw_next = final_norm_w if last else norm_w[i + 1]
        res = _out_call(h, o_hg.reshape(n, D_HG), o_pool.reshape(n, D_POOL), o_ssd.reshape(n, D_SSD_P),
                        p[i].reshape(n, P_DIM), wo[o_hg0:o_pool0].astype(BF16), wo[o_pool0:o_ssd0].astype(BF16),
                        w3.reshape(D_SSD_P, d).astype(BF16), w_pg[i].astype(BF16), w_pe[i].astype(BF16),
                        nw_next, tm, emit_h=not last)
        if last:
            out = res
        else:
            h, u = res
            u = u.reshape(b, s, d)
    return out.reshape(b, s, d)
```

```python
import functools
import math

import jax
import jax.numpy as jnp
from jax import lax
from jax.experimental import pallas as pl
from jax.experimental.pallas import tpu as pltpu

F32 = jnp.float32
BF16 = jnp.bfloat16

CHUNK = 64
EPS = 1e-6
D_MODEL = 1024
P_DIM = 256
HG_HEADS = 6
HG_D = 128
HG_PAIR = 2
HG_LEVELS = (64, 32, 16, 8, 4)
HG_UNROLL = 16
LOG2E = 1.4426950408889634
D_HG = HG_HEADS * HG_D
POOL_WINDOWS = (2, 4, 8, 16)
POOL_CH = 128
D_POOL = len(POOL_WINDOWS) * POOL_CH
POOL_HALO = 16
SSD_HEADS = 12
SSD_HEAD_DIM = 64
SSD_GROUPS = 4
SSD_HPG = SSD_HEADS // SSD_GROUPS
SSD_GW = SSD_HPG * SSD_HEAD_DIM
SSD_GWP = 256
D_SSD = SSD_HEADS * SSD_HEAD_DIM
D_SSD_P = SSD_GROUPS * SSD_GWP
SSD_STATE = 128
SSD_CONV = 4
CONV_HALO = 8
SSD_UNROLL = 4
LANES = 128
SSD_XBC_P = D_SSD_P + 2 * SSD_GROUPS * SSD_STATE
SSD_COLS = SSD_XBC_P + D_SSD_P + LANES
VMEM_LIMIT = 48 * 1024 * 1024


def _dot(a, b):
    return jnp.dot(a.astype(BF16), b.astype(BF16), preferred_element_type=F32)


def _dot_nt(a, b):
    return lax.dot_general(a.astype(BF16), b.astype(BF16), (((1,), (1,)), ((), ())),
                           preferred_element_type=F32)


def _dot_tn(a, b):
    return lax.dot_general(a.astype(BF16), b.astype(BF16), (((0,), (0,)), ((), ())),
                           preferred_element_type=F32)


def _split3(x):
    hi = x.astype(BF16)
    r1 = x - hi.astype(F32)
    mid = r1.astype(BF16)
    lo = (r1 - mid.astype(F32)).astype(BF16)
    return hi, mid, lo


def _sel_left(m, x):
    hi, mid, lo = _split3(x)
    d = lambda p: jnp.dot(m, p, preferred_element_type=F32)
    return d(hi) + (d(mid) + d(lo))


def _sel_left3(m3, x):
    return jnp.dot(m3, jnp.concatenate(_split3(x), axis=0), preferred_element_type=F32)


def _sel_right3(x, m3):
    return jnp.dot(jnp.concatenate(_split3(x), axis=1), m3, preferred_element_type=F32)


def _sigmoid(x):
    return 1.0 / (1.0 + jnp.exp(-x))


def _silu(x):
    return x * _sigmoid(x)


def _softplus(x):
    return jnp.maximum(x, 0.0) + jnp.log1p(jnp.exp(-jnp.abs(x)))


def _norm_kernel(x_ref, w_ref, o_ref):
    x = x_ref[...]
    y = x * lax.rsqrt(jnp.mean(x * x, axis=-1, keepdims=True) + EPS)
    o_ref[...] = (y * w_ref[...]).astype(o_ref.dtype)


def _rms_norm_call(x2d, w, tm):
    n, d = x2d.shape
    return pl.pallas_call(
        _norm_kernel,
        grid=(n // tm,),
        in_specs=[pl.BlockSpec((tm, d), lambda i: (i, 0)),
                  pl.BlockSpec((1, d), lambda i: (0, 0))],
        out_specs=pl.BlockSpec((tm, d), lambda i: (i, 0)),
        out_shape=jax.ShapeDtypeStruct((n, d), BF16),
        compiler_params=pltpu.CompilerParams(dimension_semantics=("parallel",)),
        name="rms_norm_in",
    )(x2d, w.reshape(1, d))


def _hgrn_kernel(u_ref, wq_ref, wf_ref, wv_ref, wg_ref, loglb_ref, log1mlb_ref, nw_ref, o_ref,
                 proj_ref, cum_ref, pair_ref, state_ref, *, n_chunks, unroll):
    @pl.when(pl.program_id(2) == 0)
    def _():
        state_ref[...] = jnp.zeros_like(state_ref)

    pw = HG_PAIR * HG_D
    u = u_ref[0]
    for kind, w_ref in enumerate((wq_ref, wf_ref, wv_ref, wg_ref)):
        proj_ref[:, kind * pw:(kind + 1) * pw] = jnp.dot(u, w_ref[...], preferred_element_type=F32)

    row = lax.broadcasted_iota(jnp.int32, (CHUNK, HG_D), 0)
    sub = lax.broadcasted_iota(jnp.int32, (CHUNK // 8, 8, HG_D), 1)
    r64 = lax.broadcasted_iota(jnp.int32, (CHUNK, CHUNK), 0)
    c64 = lax.broadcasted_iota(jnp.int32, (CHUNK, CHUNK), 1)
    r3 = lax.broadcasted_iota(jnp.int32, (CHUNK, 3 * CHUNK), 0)
    c3 = lax.broadcasted_iota(jnp.int32, (CHUNK, 3 * CHUNK), 1)
    tril3 = (r3 >= (c3 % CHUNK)).astype(BF16)
    level_mask = {bs: ((r64 // bs) == (c64 // bs)) & ((r64 & (bs // 2)) != 0) & ((c64 & (bs // 2)) == 0)
                  for bs in HG_LEVELS}
    level_sign = {bs: jnp.where((row & (bs // 2)) != 0, 1.0, -1.0) for bs in HG_LEVELS}
    diag0 = r64 == c64
    diag1 = (r64 == c64 + 1) & ((r64 & 1) == 1)
    odd = (row & 1) == 1

    def ref_rows(cum, bs):
        half = bs // 2
        if bs >= 8:
            c3 = cum.reshape(CHUNK // bs, bs, HG_D)
            return jnp.broadcast_to(c3[:, half - 1:half, :], c3.shape).reshape(CHUNK, HG_D)
        c3 = cum.reshape(CHUNK // 8, 8, HG_D)
        lo = jnp.broadcast_to(c3[:, 1:2, :], c3.shape)
        hi = jnp.broadcast_to(c3[:, 5:6, :], c3.shape)
        return jnp.where(sub < 4, lo, hi).reshape(CHUNK, HG_D)

    def gates(c, carry):
        rows = pl.ds(pl.multiple_of(c * CHUNK, CHUNK), CHUNK)
        for hh in range(HG_PAIR):
            lanes = slice(hh * HG_D, (hh + 1) * HG_D)
            z = proj_ref[rows, pw + hh * HG_D:pw + (hh + 1) * HG_D]
            loglb = loglb_ref[:, lanes]
            log1mlb = log1mlb_ref[:, lanes]
            zs = z * LOG2E
            log_sig = jnp.minimum(zs, 0.0) - jnp.log2(1.0 + jnp.exp2(-jnp.abs(zs)))
            b_ = log1mlb + log_sig
            log_f = jnp.maximum(loglb, b_) + jnp.log2(1.0 + jnp.exp2(-jnp.abs(loglb - b_)))
            f = jnp.exp2(log_f)
            kk = 1.0 - f
            proj_ref[rows, pw + hh * HG_D:pw + (hh + 1) * HG_D] = kk
            cum_ref[rows, lanes] = _sel_left3(tril3, log_f)
            pair_ref[rows, lanes] = jnp.where(odd, pltpu.roll(kk, 1, 0) * f, 0.0)
        return carry

    lax.fori_loop(0, n_chunks, gates, 0, unroll=unroll)

    def trip(c, carry):
        chains = [(j, hh) for j in range(unroll) for hh in range(HG_PAIR)]
        rows = [pl.ds(pl.multiple_of((c * unroll + j) * CHUNK, CHUNK), CHUNK) for j in range(unroll)]
        lanes = [slice(hh * HG_D, (hh + 1) * HG_D) for hh in range(HG_PAIR)]
        col = lambda kind, hh: slice(kind * pw + hh * HG_D, kind * pw + (hh + 1) * HG_D)
        q = {ch: proj_ref[rows[ch[0]], col(0, ch[1])] for ch in chains}
        kk = {ch: proj_ref[rows[ch[0]], col(1, ch[1])] for ch in chains}
        cum = {ch: cum_ref[rows[ch[0]], lanes[ch[1]]] for ch in chains}

        scores = {}
        for ch in chains:
            acc = None
            for bs in HG_LEVELS:
                e = jnp.exp2((cum[ch] - ref_rows(cum[ch], bs)) * level_sign[bs])
                m = jnp.where(level_mask[bs], _dot_nt(q[ch] * e, kk[ch] * e), 0.0)
                acc = m if acc is None else acc + m
            rs0 = jnp.sum(q[ch] * kk[ch], axis=1, keepdims=True)
            rs1 = jnp.sum(q[ch] * pair_ref[rows[ch[0]], lanes[ch[1]]], axis=1, keepdims=True)
            scores[ch] = acc + jnp.where(diag0, rs0, 0.0) + jnp.where(diag1, rs1, 0.0)

        inter = {}
        for ch in chains:
            j, hh = ch
            last = cum[ch][CHUNK - 1:CHUNK, :]
            st = state_ref[hh]
            inter[ch] = _dot_nt(q[ch] * jnp.exp2(cum[ch]), st)
            v = proj_ref[rows[j], col(2, hh)]
            state_ref[hh] = st * jnp.exp2(last) + _dot_tn(v, kk[ch] * jnp.exp2(last - cum[ch]))

        for ch in chains:
            j, hh = ch
            o = inter[ch] + _dot(scores[ch], proj_ref[rows[j], col(2, hh)])
            o = o * lax.rsqrt(jnp.mean(o * o, axis=-1, keepdims=True) + EPS) * nw_ref[:, lanes[hh]]
            g = proj_ref[rows[j], col(3, hh)]
            o_ref[0, rows[j], lanes[hh]] = (o * _silu(g)).astype(o_ref.dtype)
        return carry

    lax.fori_loop(0, n_chunks // unroll, trip, 0)


def _hgrn_call(u, w_hg, loglb, log1mlb, nw, t_blk):
    b, s, d = u.shape
    n_pairs = HG_HEADS // HG_PAIR
    pw = HG_PAIR * HG_D
    grid = (b, n_pairs, s // t_blk)
    vec = pl.BlockSpec((1, pw), lambda bi, hp, si: (0, hp))
    wspec = lambda kind: pl.BlockSpec((d, pw), lambda bi, hp, si: (0, kind * n_pairs + hp))
    return pl.pallas_call(
        functools.partial(_hgrn_kernel, n_chunks=t_blk // CHUNK, unroll=math.gcd(t_blk // CHUNK, HG_UNROLL)),
        grid=grid,
        in_specs=[pl.BlockSpec((1, t_blk, d), lambda bi, hp, si: (bi, si, 0)),
                  wspec(0), wspec(1), wspec(2), wspec(3),
                  vec, vec, vec],
        out_specs=pl.BlockSpec((1, t_blk, pw), lambda bi, hp, si: (bi, si, hp)),
        out_shape=jax.ShapeDtypeStruct((b, s, D_HG), BF16),
        scratch_shapes=[pltpu.VMEM((t_blk, 4 * pw), F32),
                        pltpu.VMEM((t_blk, pw), F32),
                        pltpu.VMEM((t_blk, pw), F32),
                        pltpu.VMEM((HG_PAIR, HG_D, HG_D), F32)],
        compiler_params=pltpu.CompilerParams(
            dimension_semantics=("parallel", "parallel", "arbitrary"),
            vmem_limit_bytes=VMEM_LIMIT),
        name="hgrn2_mixer",
    )(u, w_hg, w_hg, w_hg, w_hg, loglb, log1mlb, nw)


def _pool_kernel(u_ref, w_ref, pw_ref, ps_ref, o_ref, ext_ref, *, t_blk):
    si = pl.program_id(1)

    @pl.when(si == 0)
    def _():
        ext_ref[0:POOL_HALO, :] = jnp.zeros((POOL_HALO, D_POOL), F32)

    @pl.when(si > 0)
    def _():
        ext_ref[0:POOL_HALO, :] = ext_ref[t_blk:t_blk + POOL_HALO, :]

    proj = jnp.dot(u_ref[0], w_ref[...], preferred_element_type=F32)
    ext_ref[POOL_HALO:POOL_HALO + t_blk, :] = proj[:, :D_POOL]

    pos = (si * t_blk + 1 + lax.broadcasted_iota(jnp.int32, (t_blk, POOL_CH), 0)).astype(F32)
    for gi, win in enumerate(POOL_WINDOWS):
        lo = gi * POOL_CH
        acc = ext_ref[:, lo:lo + POOL_CH]
        shift = 1
        while shift < win:
            acc = acc + pltpu.roll(acc, shift, 0)
            shift *= 2
        cur = proj[:, lo:lo + POOL_CH]
        pooled = acc[POOL_HALO:, :] / jnp.minimum(pos, float(win)) - cur
        y = _dot(pooled, pw_ref[gi]) * ps_ref[:, lo:lo + POOL_CH]
        gate = proj[:, D_POOL + lo:D_POOL + lo + POOL_CH]
        o_ref[0, :, lo:lo + POOL_CH] = (y * _silu(gate)).astype(o_ref.dtype)


def _pool_call(u, w_pool, pool_w, pool_scale, t_blk):
    b, s, d = u.shape
    return pl.pallas_call(
        functools.partial(_pool_kernel, t_blk=t_blk),
        grid=(b, s // t_blk),
        in_specs=[pl.BlockSpec((1, t_blk, d), lambda bi, si: (bi, si, 0)),
                  pl.BlockSpec((d, 2 * D_POOL), lambda bi, si: (0, 0)),
                  pl.BlockSpec((len(POOL_WINDOWS), POOL_CH, POOL_CH), lambda bi, si: (0, 0, 0)),
                  pl.BlockSpec((1, D_POOL), lambda bi, si: (0, 0))],
        out_specs=pl.BlockSpec((1, t_blk, D_POOL), lambda bi, si: (bi, si, 0)),
        out_shape=jax.ShapeDtypeStruct((b, s, D_POOL), BF16),
        scratch_shapes=[pltpu.VMEM((t_blk + POOL_HALO, D_POOL), F32)],
        compiler_params=pltpu.CompilerParams(
            dimension_semantics=("parallel", "arbitrary"),
            vmem_limit_bytes=VMEM_LIMIT),
        name="pool_mixer",
    )(u, w_pool, pool_w, pool_scale)


def _ssd_kernel(u_ref, w_ref, cw_ref, cb_ref, dtb_ref, alogc_ref, dskip_ref, nw_ref,
                exp_ref, btril_ref, dup_ref, o_ref,
                ext_ref, act_ref, z_ref, xdt_ref, cume_ref, cumt_ref, state_ref, *, t_blk, unroll):
    si = pl.program_id(1)
    n_chunks = t_blk // CHUNK

    @pl.when(si == 0)
    def _():
        ext_ref[0:CONV_HALO, :] = jnp.zeros((CONV_HALO, SSD_XBC_P), F32)
        state_ref[...] = jnp.zeros_like(state_ref)

    @pl.when(si > 0)
    def _():
        ext_ref[0:CONV_HALO, :] = ext_ref[t_blk:t_blk + CONV_HALO, :]

    proj = jnp.dot(u_ref[0], w_ref[...], preferred_element_type=F32)
    ext_ref[CONV_HALO:CONV_HALO + t_blk, :] = proj[:, :SSD_XBC_P]
    z_ref[...] = proj[:, SSD_XBC_P:SSD_XBC_P + D_SSD_P]

    e = ext_ref[...]
    xc = cw_ref[3:4, :] * e[CONV_HALO:, :] + cb_ref[...]
    for j in range(1, SSD_CONV):
        xc = xc + cw_ref[3 - j:4 - j, :] * pltpu.roll(e, j, 0)[CONV_HALO:, :]
    xc = _silu(xc)
    act_ref[...] = xc

    dt = _softplus(proj[:, SSD_XBC_P + D_SSD_P:] + dtb_ref[...])
    exp3 = jnp.concatenate([exp_ref[...]] * 3, axis=0)
    xdt_ref[...] = xc[:, :D_SSD_P] * _sel_right3(dt, exp3)

    cum_c = _sel_left(btril_ref[...], dt * (-jnp.exp(alogc_ref[...]) * LOG2E))
    hi, mid, lo = _split3(cum_c)
    cume_ref[...] = jnp.dot(jnp.concatenate([hi, mid, lo], axis=1), exp3,
                            preferred_element_type=F32)
    cum_t = lax.dot_general(jnp.concatenate([hi, mid, lo], axis=0), jnp.concatenate([dup_ref[...]] * 3, axis=0),
                            (((0,), (0,)), ((), ())), preferred_element_type=F32)
    for c in range(n_chunks):
        cumt_ref[c] = cum_t[:, c * LANES:(c + 1) * LANES]

    r64 = lax.broadcasted_iota(jnp.int32, (CHUNK, CHUNK), 0)
    c64 = lax.broadcasted_iota(jnp.int32, (CHUNK, CHUNK), 1)
    causal = r64 >= c64
    r128 = lax.broadcasted_iota(jnp.int32, (CHUNK, LANES), 0)
    l128 = lax.broadcasted_iota(jnp.int32, (CHUNK, LANES), 1)
    causal2 = r128 >= (l128 & (CHUNK - 1))
    first_half = lax.broadcasted_iota(jnp.int32, (1, LANES), 1) < CHUNK
    rb = lax.broadcasted_iota(jnp.int32, (LANES, LANES), 0)
    lb = lax.broadcasted_iota(jnp.int32, (LANES, LANES), 1)
    blockdiag = (rb >= CHUNK) == (lb >= CHUNK)
    b_off = D_SSD_P
    c_off = D_SSD_P + SSD_GROUPS * SSD_STATE
    groups = range(SSD_GROUPS)

    gl = [slice(g * SSD_GWP, (g + 1) * SSD_GWP) for g in groups]
    bl = [slice(b_off + g * SSD_STATE, b_off + (g + 1) * SSD_STATE) for g in groups]
    cl = [slice(c_off + g * SSD_STATE, c_off + (g + 1) * SSD_STATE) for g in groups]

    def trip(c, carry):
        chains = [(j, g) for j in range(unroll) for g in groups]
        rows = [pl.ds(pl.multiple_of((c * unroll + j) * CHUNK, CHUNK), CHUNK) for j in range(unroll)]
        cum_t = [cumt_ref[c * unroll + j] for j in range(unroll)]

        cb2 = {}
        for j, g in chains:
            bm = act_ref[rows[j], bl[g]]
            cb2[j, g] = _dot_nt(act_ref[rows[j], cl[g]], jnp.concatenate([bm, bm], axis=0))
        y_diag = {}
        for j, g in chains:
            hd = g * SSD_HPG
            cs01 = jnp.where(first_half, cum_t[j][hd:hd + 1, :], cum_t[j][hd + 1:hd + 2, :])
            ct01 = cume_ref[rows[j], g * SSD_GWP:g * SSD_GWP + LANES]
            seg01 = jnp.exp2(jnp.where(causal2, ct01 - cs01, -jnp.inf))
            ct2 = cume_ref[rows[j], g * SSD_GWP + LANES:g * SSD_GWP + LANES + CHUNK]
            seg2 = jnp.exp2(jnp.where(causal, ct2 - cum_t[j][hd + 2:hd + 3, 0:CHUNK], -jnp.inf))
            x01 = xdt_ref[rows[j], g * SSD_GWP:g * SSD_GWP + LANES]
            xbd = jnp.where(blockdiag, jnp.concatenate([x01, x01], axis=0), 0.0)
            y01 = _dot(cb2[j, g] * seg01, xbd)
            y2 = _dot(cb2[j, g][:, :CHUNK] * seg2, xdt_ref[rows[j], g * SSD_GWP + LANES:(g + 1) * SSD_GWP])
            y_diag[j, g] = jnp.concatenate([y01, y2], axis=1)

        y_off = {}
        for j, g in chains:
            cum_e = cume_ref[rows[j], gl[g]]
            last_e = cum_e[CHUNK - 1:CHUNK, :]
            hst = state_ref[g]
            y_off[j, g] = _dot(act_ref[rows[j], cl[g]], hst) * jnp.exp2(cum_e)
            xdec = xdt_ref[rows[j], gl[g]] * jnp.exp2(last_e - cum_e)
            state_ref[g] = hst * jnp.exp2(last_e) + _dot_tn(act_ref[rows[j], bl[g]], xdec)

        for j, g in chains:
            y = y_diag[j, g] + y_off[j, g] + act_ref[rows[j], gl[g]] * dskip_ref[:, gl[g]]
            y = y * _silu(z_ref[rows[j], gl[g]])
            ms = jnp.sum(y * y, axis=-1, keepdims=True) * (1.0 / SSD_GW)
            y = y * lax.rsqrt(ms + EPS) * nw_ref[:, gl[g]]
            o_ref[0, rows[j], gl[g]] = y.astype(o_ref.dtype)
        return carry

    lax.fori_loop(0, n_chunks // unroll, trip, 0)


def _ssd_call(u, w_ssd, conv_w, conv_b, dt_bias, alog_c, dskip_e, nw_e, t_blk):
    b, s, d = u.shape
    n_chunks = t_blk // CHUNK
    heads = jnp.arange(SSD_HEADS)
    lanes_e = (heads // SSD_HPG) * SSD_GWP + (heads % SSD_HPG) * SSD_HEAD_DIM
    col_e = jnp.arange(D_SSD_P)[None, :]
    expand = ((col_e >= lanes_e[:, None]) & (col_e < lanes_e[:, None] + SSD_HEAD_DIM))
    expand = jnp.pad(expand, ((0, LANES - SSD_HEADS), (0, 0))).astype(BF16)
    ti = jnp.arange(t_blk)
    btril = ((ti[:, None] >= ti[None, :]) & (ti[:, None] // CHUNK == ti[None, :] // CHUNK)).astype(BF16)
    cj = jnp.arange(2 * t_blk)
    dup = (ti[:, None] == (cj[None, :] // LANES) * CHUNK + cj[None, :] % CHUNK).astype(BF16)

    full = lambda shape: pl.BlockSpec(shape, lambda bi, si: (0,) * len(shape))
    return pl.pallas_call(
        functools.partial(_ssd_kernel, t_blk=t_blk, unroll=math.gcd(n_chunks, SSD_UNROLL)),
        grid=(b, s // t_blk),
        in_specs=[pl.BlockSpec((1, t_blk, d), lambda bi, si: (bi, si, 0)),
                  full((d, SSD_COLS)),
                  full((SSD_CONV, SSD_XBC_P)), full((1, SSD_XBC_P)),
                  full((1, LANES)), full((1, LANES)),
                  full((1, D_SSD_P)), full((1, D_SSD_P)),
                  full((LANES, D_SSD_P)), full((t_blk, t_blk)), full((t_blk, 2 * t_blk))],
        out_specs=pl.BlockSpec((1, t_blk, D_SSD_P), lambda bi, si: (bi, si, 0)),
        out_shape=jax.ShapeDtypeStruct((b, s, D_SSD_P), BF16),
        scratch_shapes=[pltpu.VMEM((t_blk + CONV_HALO, SSD_XBC_P), F32),
                        pltpu.VMEM((t_blk, SSD_XBC_P), F32),
                        pltpu.VMEM((t_blk, D_SSD_P), F32),
                        pltpu.VMEM((t_blk, D_SSD_P), F32),
                        pltpu.VMEM((t_blk, D_SSD_P), F32),
                        pltpu.VMEM((n_chunks, LANES, LANES), F32),
                        pltpu.VMEM((SSD_GROUPS, SSD_STATE, SSD_GWP), F32)],
        compiler_params=pltpu.CompilerParams(
            dimension_semantics=("parallel", "arbitrary"),
            vmem_limit_bytes=VMEM_LIMIT),
        name="ssd_mixer",
    )(u, w_ssd, conv_w, conv_b, dt_bias, alog_c, dskip_e, nw_e, expand, btril, dup)


def _out_kernel(h_ref, ohg_ref, opool_ref, ossd_ref, p_ref, w1_ref, w2_ref, w3_ref, wpg_ref, wpe_ref,
                nw_ref, *out_refs, emit_h):
    dot = lambda a, w: jnp.dot(a, w, preferred_element_type=F32)
    acc = h_ref[...] + dot(ohg_ref[...], w1_ref[...]) + dot(opool_ref[...], w2_ref[...]) \
        + dot(ossd_ref[...], w3_ref[...])
    gate = _sigmoid(_dot(acc, wpg_ref[...]))
    hn = acc + gate * _dot(p_ref[...], wpe_ref[...])
    y = hn * lax.rsqrt(jnp.mean(hn * hn, axis=-1, keepdims=True) + EPS) * nw_ref[...]
    if emit_h:
        out_refs[0][...] = hn
        out_refs[1][...] = y.astype(out_refs[1].dtype)
    else:
        out_refs[0][...] = y


def _out_call(h2d, ohg, opool, ossd, p2d, w1, w2, w3, wpg, wpe, nw, tm, emit_h):
    n, d = h2d.shape
    rows = lambda width: pl.BlockSpec((tm, width), lambda i: (i, 0))
    full = lambda a: pl.BlockSpec(a.shape, lambda i: (0, 0))
    if emit_h:
        out_shape = (jax.ShapeDtypeStruct((n, d), F32), jax.ShapeDtypeStruct((n, d), BF16))
        out_specs = (rows(d), rows(d))
    else:
        out_shape = jax.ShapeDtypeStruct((n, d), F32)
        out_specs = rows(d)
    return pl.pallas_call(
        functools.partial(_out_kernel, emit_h=emit_h),
        grid=(n // tm,),
        in_specs=[rows(d), rows(D_HG), rows(D_POOL), rows(D_SSD_P), rows(P_DIM),
                  full(w1), full(w2), full(w3), full(wpg), full(wpe), pl.BlockSpec((1, d), lambda i: (0, 0))],
        out_specs=out_specs,
        out_shape=out_shape,
        compiler_params=pltpu.CompilerParams(
            dimension_semantics=("parallel",), vmem_limit_bytes=VMEM_LIMIT),
        name="out_proj",
    )(h2d, ohg, opool, ossd, p2d, w1, w2, w3, wpg, wpe, nw.reshape(1, d))


def _pad_groups(a):
    lead = a.shape[:-1]
    a = a.reshape(lead + (SSD_GROUPS, SSD_GW))
    a = jnp.pad(a, [(0, 0)] * len(lead) + [(0, 0), (0, SSD_GWP - SSD_GW)])
    return a.reshape(lead + (D_SSD_P,))


def _pad_lanes(a):
    return jnp.pad(a, [(0, 0)] * (a.ndim - 1) + [(0, LANES - a.shape[-1])])


C_POOL = 4 * D_HG
C_XBC = C_POOL + 2 * D_POOL
C_B = C_XBC + D_SSD
C_C = C_B + SSD_GROUPS * SSD_STATE
C_DT = C_C + SSD_GROUPS * SSD_STATE
C_Z = C_DT + SSD_HEADS
N_IN = C_Z + D_SSD
N_IN_MAIN = (N_IN // LANES) * LANES
WPREP_ROWS = 256


def _wprep_kernel(w_ref, tail_ref, hg_ref, pool_ref, ssd_ref):
    rows = w_ref.shape[1]
    lane = lax.broadcasted_iota(jnp.int32, (rows, LANES), 1)
    tail = jnp.where(lane < N_IN - N_IN_MAIN, tail_ref[0], 0.0)
    hg_ref[0] = w_ref[0, :, 0:C_POOL].astype(BF16)
    pool_ref[0] = w_ref[0, :, C_POOL:C_XBC].astype(BF16)

    def window(a0, width):
        if a0 + width <= N_IN_MAIN:
            return w_ref[0, :, a0:a0 + width]
        assert a0 + width == N_IN_MAIN + LANES
        return jnp.concatenate([w_ref[0, :, a0:N_IN_MAIN], tail], axis=1)

    lane_g = lax.broadcasted_iota(jnp.int32, (rows, SSD_GWP), 1)

    def group(a):
        a0 = (a // LANES) * LANES
        win = window(a0, SSD_GWP + LANES)
        if a != a0:
            win = pltpu.roll(win, SSD_GWP + LANES - (a - a0), 1)
        return jnp.where(lane_g < SSD_GW, win[:, :SSD_GWP], 0.0)

    for g in range(SSD_GROUPS):
        ssd_ref[0, :, g * SSD_GWP:(g + 1) * SSD_GWP] = group(C_XBC + g * SSD_GW).astype(BF16)
        zo = SSD_XBC_P + g * SSD_GWP
        ssd_ref[0, :, zo:zo + SSD_GWP] = group(C_Z + g * SSD_GW).astype(BF16)
    ssd_ref[0, :, D_SSD_P:SSD_XBC_P] = w_ref[0, :, C_B:C_DT].astype(BF16)
    dt = jnp.where(lane < SSD_HEADS, window(C_DT, LANES), 0.0)
    ssd_ref[0, :, SSD_XBC_P + D_SSD_P:] = dt.astype(BF16)


def _wprep_call(w_in):
    depth, d, n_in = w_in.shape
    assert n_in == N_IN and C_DT % LANES == 0 and d % WPREP_ROWS == 0
    out = lambda width: pl.BlockSpec((1, WPREP_ROWS, width), lambda i, r: (i, r, 0))
    return pl.pallas_call(
        _wprep_kernel,
        grid=(depth, d // WPREP_ROWS),
        in_specs=[pl.BlockSpec((1, WPREP_ROWS, N_IN_MAIN), lambda i, r: (i, r, 0)),
                  pl.BlockSpec((1, WPREP_ROWS, LANES), lambda i, r: (i, r, N_IN_MAIN // LANES))],
        out_specs=(out(C_POOL), out(C_XBC - C_POOL), out(SSD_COLS)),
        out_shape=(jax.ShapeDtypeStruct((depth, d, C_POOL), BF16),
                   jax.ShapeDtypeStruct((depth, d, C_XBC - C_POOL), BF16),
                   jax.ShapeDtypeStruct((depth, d, SSD_COLS), BF16)),
        compiler_params=pltpu.CompilerParams(
            dimension_semantics=("parallel", "parallel"), vmem_limit_bytes=VMEM_LIMIT),
        name="weight_prep",
    )(w_in, w_in)


def kernel(x, p, norm_w, w_in, hg_lb, hg_norm_w, pool_w, pool_scale, conv_w, conv_b, dt_bias, a_log, d_skip,
           ssd_norm_w, w_out, w_pe, w_pg, final_norm_w):
    b, s, d = x.shape
    depth = w_in.shape[0]
    n = b * s
    t_hg = min(1024, s)
    t_pool = min(1024, s)
    t_ssd = min(256, s)
    tm = min(512, n)

    lb_all = jnp.cumsum(jax.nn.softmax(hg_lb.astype(F32), axis=0), axis=0)
    lb_all = lb_all - lb_all[0]

    o_hg0, o_pool0 = 0, D_HG
    o_ssd0 = D_HG + D_POOL

    w_hg_all, w_pool_all, w_ssd_all = _wprep_call(w_in)

    h = x.reshape(n, d)
    u = _rms_norm_call(h, norm_w[0], tm).reshape(b, s, d)
    out = None
    for i in range(depth):
        lb = lb_all[i].reshape(1, D_HG)
        o_hg = _hgrn_call(u, w_hg_all[i], jnp.log(lb) * LOG2E, jnp.log1p(-lb) * LOG2E,
                          hg_norm_w[i].reshape(1, D_HG), t_hg)
        o_pool = _pool_call(u, w_pool_all[i], pool_w[i].astype(BF16),
                            pool_scale[i].reshape(1, D_POOL), t_pool)
        xbc_split = lambda a: jnp.concatenate([_pad_groups(a[..., :D_SSD]), a[..., D_SSD:]], axis=-1)
        rep = lambda a: _pad_groups(jnp.repeat(a, SSD_HEAD_DIM)).reshape(1, D_SSD_P)
        o_ssd = _ssd_call(u, w_ssd_all[i], xbc_split(conv_w[i]), xbc_split(conv_b[i]).reshape(1, SSD_XBC_P),
                          _pad_lanes(dt_bias[i]).reshape(1, LANES), _pad_lanes(a_log[i]).reshape(1, LANES),
                          rep(d_skip[i]), _pad_groups(ssd_norm_w[i]).reshape(1, D_SSD_P), t_ssd)
        wo = w_out[i]
        w3 = jnp.pad(wo[o_ssd0:].reshape(SSD_GROUPS, SSD_GW, d), ((0, 0), (0, SSD_GWP - SSD_GW), (0, 0)))
        last = i == depth - 1
        nw_next = final_norm_w if last else norm_w[i + 1]
        res = _out_call(h, o_hg.reshape(n, D_HG), o_pool.reshape(n, D_POOL), o_ssd.reshape(n, D_SSD_P),
                        p[i].reshape(n, P_DIM), wo[o_hg0:o_pool0].astype(BF16), wo[o_pool0:o_ssd0].astype(BF16),
                        w3.reshape(D_SSD_P, d).astype(BF16), w_pg[i].astype(BF16), w_pe[i].astype(BF16),
                        nw_next, tm, emit_h=not last)
        if last:
            out = res
        else:
            h, u = res
            u = u.reshape(b, s, d)
    return out.reshape(b, s, d)
```

```python
import functools
import math

import jax
import jax.numpy as jnp
from jax import lax
from jax.experimental import pallas as pl
from jax.experimental.pallas import tpu as pltpu

F32 = jnp.float32
BF16 = jnp.bfloat16

CHUNK = 64
EPS = 1e-6
D_MODEL = 1024
P_DIM = 256
HG_HEADS = 6
HG_D = 128
HG_PAIR = 2
HG_LEVELS = (64, 32, 16, 8, 4)
HG_HEAD_ROWS = 256
HG_GROUP = 16
HG_UNROLL = 16
LOG2E = 1.4426950408889634
D_HG = HG_HEADS * HG_D
POOL_WINDOWS = (2, 4, 8, 16)
POOL_CH = 128
D_POOL = len(POOL_WINDOWS) * POOL_CH
POOL_HALO = 16
SSD_HEADS = 12
SSD_HEAD_DIM = 64
SSD_GROUPS = 4
SSD_HPG = SSD_HEADS // SSD_GROUPS
SSD_GW = SSD_HPG * SSD_HEAD_DIM
SSD_GWP = 256
D_SSD = SSD_HEADS * SSD_HEAD_DIM
D_SSD_P = SSD_GROUPS * SSD_GWP
SSD_STATE = 128
SSD_CONV = 4
CONV_HALO = 8
SSD_SEL_ROWS = 256
SSD_UNROLL = 8
LANES = 128
SSD_XBC_P = D_SSD_P + 2 * SSD_GROUPS * SSD_STATE
SSD_COLS = SSD_XBC_P + D_SSD_P + LANES
VMEM_LIMIT = 48 * 1024 * 1024


def _dot(a, b):
    return jnp.dot(a.astype(BF16), b.astype(BF16), preferred_element_type=F32)


def _dot_nt(a, b):
    return lax.dot_general(a.astype(BF16), b.astype(BF16), (((1,), (1,)), ((), ())),
                           preferred_element_type=F32)


def _dot_tn(a, b):
    return lax.dot_general(a.astype(BF16), b.astype(BF16), (((0,), (0,)), ((), ())),
                           preferred_element_type=F32)


def _split3(x):
    hi = x.astype(BF16)
    r1 = x - hi.astype(F32)
    mid = r1.astype(BF16)
    lo = (r1 - mid.astype(F32)).astype(BF16)
    return hi, mid, lo


def _sel_left(m, x):
    hi, mid, lo = _split3(x)
    d = lambda p: jnp.dot(m, p, preferred_element_type=F32)
    return d(hi) + (d(mid) + d(lo))


def _sel_left3(m3, x):
    return jnp.dot(m3, jnp.concatenate(_split3(x), axis=0), preferred_element_type=F32)


def _sel_right3(x, m3):
    return jnp.dot(jnp.concatenate(_split3(x), axis=1), m3, preferred_element_type=F32)


def _chunk_rows(i):
    if isinstance(i, int):
        return pl.ds(i * CHUNK, CHUNK)
    return pl.ds(pl.multiple_of(i * CHUNK, CHUNK), CHUNK)


def _run_trips(n_trips, body):
    if n_trips == 1:
        body(0, 0)
    else:
        lax.fori_loop(0, n_trips, body, 0)


def _sigmoid(x):
    return 1.0 / (1.0 + jnp.exp(-x))


def _silu(x):
    return x * _sigmoid(x)


def _softplus(x):
    return jnp.maximum(x, 0.0) + jnp.log1p(jnp.exp(-jnp.abs(x)))


def _norm_kernel(x_ref, w_ref, o_ref):
    x = x_ref[...]
    y = x * lax.rsqrt(jnp.mean(x * x, axis=-1, keepdims=True) + EPS)
    o_ref[...] = (y * w_ref[...]).astype(o_ref.dtype)


def _rms_norm_call(x2d, w, tm):
    n, d = x2d.shape
    return pl.pallas_call(
        _norm_kernel,
        grid=(n // tm,),
        in_specs=[pl.BlockSpec((tm, d), lambda i: (i, 0)),
                  pl.BlockSpec((1, d), lambda i: (0, 0))],
        out_specs=pl.BlockSpec((tm, d), lambda i: (i, 0)),
        out_shape=jax.ShapeDtypeStruct((n, d), BF16),
        compiler_params=pltpu.CompilerParams(dimension_semantics=("parallel",)),
        name="rms_norm_in",
    )(x2d, w.reshape(1, d))


def _hgrn_kernel(u_ref, wq_ref, wf_ref, wv_ref, wg_ref, loglb_ref, log1mlb_ref, nw_ref, o_ref,
                 proj_ref, vg_ref, cum_ref, pair_ref, state_ref, *, n_chunks, unroll):
    @pl.when(pl.program_id(2) == 0)
    def _():
        state_ref[...] = jnp.zeros_like(state_ref)

    pw = HG_PAIR * HG_D
    w_refs = (wq_ref, wf_ref, wv_ref, wg_ref)

    def project(kind, r0, r1):
        dst, slot = (proj_ref, kind) if kind < 2 else (vg_ref, kind - 2)
        dst[r0:r1, slot * pw:(slot + 1) * pw] = jnp.dot(u_ref[0, r0:r1, :], w_refs[kind][...],
                                                        preferred_element_type=F32)

    t_blk = n_chunks * CHUNK
    for r0 in range(0, t_blk, HG_HEAD_ROWS):
        project(1, r0, min(r0 + HG_HEAD_ROWS, t_blk))
    for kind in (0, 2, 3):
        project(kind, 0, t_blk)

    row = lax.broadcasted_iota(jnp.int32, (CHUNK, HG_D), 0)
    sub = lax.broadcasted_iota(jnp.int32, (CHUNK // 8, 8, HG_D), 1)
    r64 = lax.broadcasted_iota(jnp.int32, (CHUNK, CHUNK), 0)
    c64 = lax.broadcasted_iota(jnp.int32, (CHUNK, CHUNK), 1)
    r3 = lax.broadcasted_iota(jnp.int32, (CHUNK, 3 * CHUNK), 0)
    c3 = lax.broadcasted_iota(jnp.int32, (CHUNK, 3 * CHUNK), 1)
    tril3 = (r3 >= (c3 % CHUNK)).astype(BF16)
    level_mask = {bs: ((r64 // bs) == (c64 // bs)) & ((r64 & (bs // 2)) != 0) & ((c64 & (bs // 2)) == 0)
                  for bs in HG_LEVELS}
    level_sign = {bs: jnp.where((row & (bs // 2)) != 0, 1.0, -1.0) for bs in HG_LEVELS}
    diag0 = r64 == c64
    diag1 = (r64 == c64 + 1) & ((r64 & 1) == 1)
    odd = (row & 1) == 1

    def ref_rows(cum, bs):
        half = bs // 2
        if bs >= 8:
            c3 = cum.reshape(CHUNK // bs, bs, HG_D)
            return jnp.broadcast_to(c3[:, half - 1:half, :], c3.shape).reshape(CHUNK, HG_D)
        c3 = cum.reshape(CHUNK // 8, 8, HG_D)
        lo = jnp.broadcast_to(c3[:, 1:2, :], c3.shape)
        hi = jnp.broadcast_to(c3[:, 5:6, :], c3.shape)
        return jnp.where(sub < 4, lo, hi).reshape(CHUNK, HG_D)

    def gates(c, carry):
        rows = _chunk_rows(c)
        for hh in range(HG_PAIR):
            lanes = slice(hh * HG_D, (hh + 1) * HG_D)
            z = proj_ref[rows, pw + hh * HG_D:pw + (hh + 1) * HG_D]
            loglb = loglb_ref[:, lanes]
            log1mlb = log1mlb_ref[:, lanes]
            zs = z * LOG2E
            log_sig = jnp.minimum(zs, 0.0) - jnp.log2(1.0 + jnp.exp2(-jnp.abs(zs)))
            b_ = log1mlb + log_sig
            log_f = jnp.maximum(loglb, b_) + jnp.log2(1.0 + jnp.exp2(-jnp.abs(loglb - b_)))
            f = jnp.exp2(log_f)
            kk = 1.0 - f
            proj_ref[rows, pw + hh * HG_D:pw + (hh + 1) * HG_D] = kk
            cum_ref[rows, lanes] = _sel_left3(tril3, log_f)
            pair_ref[rows, lanes] = jnp.where(odd, pltpu.roll(kk, 1, 0) * f, 0.0)
        return carry

    _run_trips(n_chunks // unroll, lambda c, carry: [gates(c * unroll + j, carry) for j in range(unroll)][-1])

    def trip(c, carry):
        for j0 in range(0, unroll, min(unroll, HG_GROUP)):
            stage_group(c, range(j0, j0 + min(unroll, HG_GROUP)))
        return carry

    def stage_group(c, chunk_ids):
        chains = [(j, hh) for j in chunk_ids for hh in range(HG_PAIR)]
        rows = {j: _chunk_rows(c * unroll + j) for j in chunk_ids}
        lanes = [slice(hh * HG_D, (hh + 1) * HG_D) for hh in range(HG_PAIR)]
        col = lambda kind, hh: slice(kind * pw + hh * HG_D, kind * pw + (hh + 1) * HG_D)
        q = {ch: proj_ref[rows[ch[0]], col(0, ch[1])] for ch in chains}
        kk = {ch: proj_ref[rows[ch[0]], col(1, ch[1])] for ch in chains}
        cum = {ch: cum_ref[rows[ch[0]], lanes[ch[1]]] for ch in chains}

        scores = {}
        for ch in chains:
            acc = None
            for bs in HG_LEVELS:
                e = jnp.exp2((cum[ch] - ref_rows(cum[ch], bs)) * level_sign[bs])
                m = jnp.where(level_mask[bs], _dot_nt(q[ch] * e, kk[ch] * e), 0.0)
                acc = m if acc is None else acc + m
            rs0 = jnp.sum(q[ch] * kk[ch], axis=1, keepdims=True)
            rs1 = jnp.sum(q[ch] * pair_ref[rows[ch[0]], lanes[ch[1]]], axis=1, keepdims=True)
            scores[ch] = acc + jnp.where(diag0, rs0, 0.0) + jnp.where(diag1, rs1, 0.0)

        inter = {}
        for ch in chains:
            j, hh = ch
            last = cum[ch][CHUNK - 1:CHUNK, :]
            st = state_ref[hh]
            inter[ch] = _dot_nt(q[ch] * jnp.exp2(cum[ch]), st)
            v = vg_ref[rows[j], col(0, hh)]
            state_ref[hh] = st * jnp.exp2(last) + _dot_tn(v, kk[ch] * jnp.exp2(last - cum[ch]))

        for ch in chains:
            j, hh = ch
            o = inter[ch] + _dot(scores[ch], vg_ref[rows[j], col(0, hh)])
            o = o * lax.rsqrt(jnp.mean(o * o, axis=-1, keepdims=True) + EPS) * nw_ref[:, lanes[hh]]
            g = vg_ref[rows[j], col(1, hh)]
            o_ref[0, rows[j], lanes[hh]] = (o * _silu(g)).astype(o_ref.dtype)

    _run_trips(n_chunks // unroll, trip)


def _hgrn_call(u, w_hg, loglb, log1mlb, nw, t_blk):
    b, s, d = u.shape
    n_pairs = HG_HEADS // HG_PAIR
    pw = HG_PAIR * HG_D
    grid = (b, n_pairs, s // t_blk)
    vec = pl.BlockSpec((1, pw), lambda bi, hp, si: (0, hp))
    wspec = lambda kind: pl.BlockSpec((d, pw), lambda bi, hp, si: (0, kind * n_pairs + hp))
    return pl.pallas_call(
        functools.partial(_hgrn_kernel, n_chunks=t_blk // CHUNK, unroll=math.gcd(t_blk // CHUNK, HG_UNROLL)),
        grid=grid,
        in_specs=[pl.BlockSpec((1, t_blk, d), lambda bi, hp, si: (bi, si, 0)),
                  wspec(0), wspec(1), wspec(2), wspec(3),
                  vec, vec, vec],
        out_specs=pl.BlockSpec((1, t_blk, pw), lambda bi, hp, si: (bi, si, hp)),
        out_shape=jax.ShapeDtypeStruct((b, s, D_HG), BF16),
        scratch_shapes=[pltpu.VMEM((t_blk, 2 * pw), F32),
                        pltpu.VMEM((t_blk, 2 * pw), F32),
                        pltpu.VMEM((t_blk, pw), F32),
                        pltpu.VMEM((t_blk, pw), F32),
                        pltpu.VMEM((HG_PAIR, HG_D, HG_D), F32)],
        compiler_params=pltpu.CompilerParams(
            dimension_semantics=("parallel", "parallel", "arbitrary"),
            vmem_limit_bytes=VMEM_LIMIT),
        name="hgrn2_mixer",
    )(u, w_hg, w_hg, w_hg, w_hg, loglb, log1mlb, nw)


def _pool_kernel(u_ref, w_ref, pw_ref, ps_ref, o_ref, ext_ref, *, t_blk):
    si = pl.program_id(1)

    @pl.when(si == 0)
    def _():
        ext_ref[0:POOL_HALO, :] = jnp.zeros((POOL_HALO, D_POOL), F32)

    @pl.when(si > 0)
    def _():
        ext_ref[0:POOL_HALO, :] = ext_ref[t_blk:t_blk + POOL_HALO, :]

    proj = jnp.dot(u_ref[0], w_ref[...], preferred_element_type=F32)
    ext_ref[POOL_HALO:POOL_HALO + t_blk, :] = proj[:, :D_POOL]

    pos = (si * t_blk + 1 + lax.broadcasted_iota(jnp.int32, (t_blk, POOL_CH), 0)).astype(F32)
    for gi, win in enumerate(POOL_WINDOWS):
        lo = gi * POOL_CH
        acc = ext_ref[:, lo:lo + POOL_CH]
        shift = 1
        while shift < win:
            acc = acc + pltpu.roll(acc, shift, 0)
            shift *= 2
        cur = proj[:, lo:lo + POOL_CH]
        pooled = acc[POOL_HALO:, :] / jnp.minimum(pos, float(win)) - cur
        y = _dot(pooled, pw_ref[gi]) * ps_ref[:, lo:lo + POOL_CH]
        gate = proj[:, D_POOL + lo:D_POOL + lo + POOL_CH]
        o_ref[0, :, lo:lo + POOL_CH] = (y * _silu(gate)).astype(o_ref.dtype)


def _pool_call(u, w_pool, pool_w, pool_scale, t_blk):
    b, s, d = u.shape
    return pl.pallas_call(
        functools.partial(_pool_kernel, t_blk=t_blk),
        grid=(b, s // t_blk),
        in_specs=[pl.BlockSpec((1, t_blk, d), lambda bi, si: (bi, si, 0)),
                  pl.BlockSpec((d, 2 * D_POOL), lambda bi, si: (0, 0)),
                  pl.BlockSpec((len(POOL_WINDOWS), POOL_CH, POOL_CH), lambda bi, si: (0, 0, 0)),
                  pl.BlockSpec((1, D_POOL), lambda bi, si: (0, 0))],
        out_specs=pl.BlockSpec((1, t_blk, D_POOL), lambda bi, si: (bi, si, 0)),
        out_shape=jax.ShapeDtypeStruct((b, s, D_POOL), BF16),
        scratch_shapes=[pltpu.VMEM((t_blk + POOL_HALO, D_POOL), F32)],
        compiler_params=pltpu.CompilerParams(
            dimension_semantics=("parallel", "arbitrary"),
            vmem_limit_bytes=VMEM_LIMIT),
        name="pool_mixer",
    )(u, w_pool, pool_w, pool_scale)


def _ssd_kernel(u_ref, w_ref, cw_ref, cb_ref, dtb_ref, alogc_ref, dskip_ref, nw_ref,
                exp_ref, btril_ref, dup_ref, o_ref,
                ext_ref, act_ref, z_ref, xdt_ref, cume_ref, cumt_ref, state_ref, *, t_blk, unroll):
    si = pl.program_id(1)
    n_chunks = t_blk // CHUNK

    @pl.when(si == 0)
    def _():
        ext_ref[0:CONV_HALO, :] = jnp.zeros((CONV_HALO, SSD_XBC_P), F32)
        state_ref[...] = jnp.zeros_like(state_ref)

    @pl.when(si > 0)
    def _():
        ext_ref[0:CONV_HALO, :] = ext_ref[t_blk:t_blk + CONV_HALO, :]

    proj = jnp.dot(u_ref[0], w_ref[...], preferred_element_type=F32)
    ext_ref[CONV_HALO:CONV_HALO + t_blk, :] = proj[:, :SSD_XBC_P]
    z_ref[...] = proj[:, SSD_XBC_P:SSD_XBC_P + D_SSD_P]

    assert SSD_CONV == 4
    e = ext_ref[...]
    e2 = pltpu.roll(e, 2, 0)
    even = cw_ref[3:4, :] * e + cw_ref[1:2, :] * e2
    odd = cw_ref[2:3, :] * e + cw_ref[0:1, :] * e2
    xc = (even + pltpu.roll(odd, 1, 0))[CONV_HALO:, :] + cb_ref[...]
    xc = _silu(xc)
    act_ref[...] = xc

    dt = _softplus(proj[:, SSD_XBC_P + D_SSD_P:] + dtb_ref[...])
    exp3 = jnp.concatenate([exp_ref[...]] * 3, axis=0)
    xdt_ref[...] = xc[:, :D_SSD_P] * _sel_right3(dt, exp3)

    dta = dt * (-jnp.exp(alogc_ref[...]) * LOG2E)
    sub = btril_ref.shape[0]
    dup3 = jnp.concatenate([dup_ref[...]] * 3, axis=0)
    for r0 in range(0, t_blk, sub):
        cum_c = _sel_left(btril_ref[...], dta[r0:r0 + sub, :])
        hi, mid, lo = _split3(cum_c)
        cume_ref[r0:r0 + sub, :] = jnp.dot(jnp.concatenate([hi, mid, lo], axis=1), exp3,
                                           preferred_element_type=F32)
        cum_t = lax.dot_general(jnp.concatenate([hi, mid, lo], axis=0), dup3,
                                (((0,), (0,)), ((), ())), preferred_element_type=F32)
        for c in range(sub // CHUNK):
            cumt_ref[r0 // CHUNK + c] = cum_t[:, c * LANES:(c + 1) * LANES]

    r64 = lax.broadcasted_iota(jnp.int32, (CHUNK, CHUNK), 0)
    c64 = lax.broadcasted_iota(jnp.int32, (CHUNK, CHUNK), 1)
    causal = r64 >= c64
    r128 = lax.broadcasted_iota(jnp.int32, (CHUNK, LANES), 0)
    l128 = lax.broadcasted_iota(jnp.int32, (CHUNK, LANES), 1)
    causal2 = r128 >= (l128 & (CHUNK - 1))
    first_half = lax.broadcasted_iota(jnp.int32, (1, LANES), 1) < CHUNK
    rb = lax.broadcasted_iota(jnp.int32, (LANES, LANES), 0)
    lb = lax.broadcasted_iota(jnp.int32, (LANES, LANES), 1)
    blockdiag = (rb >= CHUNK) == (lb >= CHUNK)
    b_off = D_SSD_P
    c_off = D_SSD_P + SSD_GROUPS * SSD_STATE
    groups = range(SSD_GROUPS)

    gl = [slice(g * SSD_GWP, (g + 1) * SSD_GWP) for g in groups]
    bl = [slice(b_off + g * SSD_STATE, b_off + (g + 1) * SSD_STATE) for g in groups]
    cl = [slice(c_off + g * SSD_STATE, c_off + (g + 1) * SSD_STATE) for g in groups]

    def trip(c, carry):
        chains = [(j, g) for j in range(unroll) for g in groups]
        rows = [_chunk_rows(c * unroll + j) for j in range(unroll)]
        cum_t = [cumt_ref[c * unroll + j] for j in range(unroll)]

        cb2 = {}
        for j, g in chains:
            bm = act_ref[rows[j], bl[g]]
            cb2[j, g] = _dot_nt(act_ref[rows[j], cl[g]], jnp.concatenate([bm, bm], axis=0))
        y_diag = {}
        for j, g in chains:
            hd = g * SSD_HPG
            cs01 = jnp.where(first_half, cum_t[j][hd:hd + 1, :], cum_t[j][hd + 1:hd + 2, :])
            ct01 = cume_ref[rows[j], g * SSD_GWP:g * SSD_GWP + LANES]
            seg01 = jnp.exp2(jnp.where(causal2, ct01 - cs01, -jnp.inf))
            ct2 = cume_ref[rows[j], g * SSD_GWP + LANES:g * SSD_GWP + LANES + CHUNK]
            seg2 = jnp.exp2(jnp.where(causal, ct2 - cum_t[j][hd + 2:hd + 3, 0:CHUNK], -jnp.inf))
            x01 = xdt_ref[rows[j], g * SSD_GWP:g * SSD_GWP + LANES]
            xbd = jnp.where(blockdiag, jnp.concatenate([x01, x01], axis=0), 0.0)
            y01 = _dot(cb2[j, g] * seg01, xbd)
            y2 = _dot(cb2[j, g][:, :CHUNK] * seg2, xdt_ref[rows[j], g * SSD_GWP + LANES:(g + 1) * SSD_GWP])
            y_diag[j, g] = jnp.concatenate([y01, y2], axis=1)

        y_off = {}
        for j, g in chains:
            cum_e = cume_ref[rows[j], gl[g]]
            last_e = cum_e[CHUNK - 1:CHUNK, :]
            hst = state_ref[g]
            y_off[j, g] = _dot(act_ref[rows[j], cl[g]], hst) * jnp.exp2(cum_e)
            xdec = xdt_ref[rows[j], gl[g]] * jnp.exp2(last_e - cum_e)
            state_ref[g] = hst * jnp.exp2(last_e) + _dot_tn(act_ref[rows[j], bl[g]], xdec)

        for j, g in chains:
            y = y_diag[j, g] + y_off[j, g] + act_ref[rows[j], gl[g]] * dskip_ref[:, gl[g]]
            y = y * _silu(z_ref[rows[j], gl[g]])
            ms = jnp.sum(y * y, axis=-1, keepdims=True) * (1.0 / SSD_GW)
            y = y * lax.rsqrt(ms + EPS) * nw_ref[:, gl[g]]
            o_ref[0, rows[j], gl[g]] = y.astype(o_ref.dtype)
        return carry

    _run_trips(n_chunks // unroll, trip)


def _ssd_call(u, w_ssd, conv_w, conv_b, dt_bias, alog_c, dskip_e, nw_e, t_blk):
    b, s, d = u.shape
    n_chunks = t_blk // CHUNK
    heads = jnp.arange(SSD_HEADS)
    lanes_e = (heads // SSD_HPG) * SSD_GWP + (heads % SSD_HPG) * SSD_HEAD_DIM
    col_e = jnp.arange(D_SSD_P)[None, :]
    expand = ((col_e >= lanes_e[:, None]) & (col_e < lanes_e[:, None] + SSD_HEAD_DIM))
    expand = jnp.pad(expand, ((0, LANES - SSD_HEADS), (0, 0))).astype(BF16)
    sub = math.gcd(t_blk, SSD_SEL_ROWS)
    ti = jnp.arange(sub)
    btril = ((ti[:, None] >= ti[None, :]) & (ti[:, None] // CHUNK == ti[None, :] // CHUNK)).astype(BF16)
    cj = jnp.arange(2 * sub)
    dup = (ti[:, None] == (cj[None, :] // LANES) * CHUNK + cj[None, :] % CHUNK).astype(BF16)

    full = lambda shape: pl.BlockSpec(shape, lambda bi, si: (0,) * len(shape))
    return pl.pallas_call(
        functools.partial(_ssd_kernel, t_blk=t_blk, unroll=math.gcd(n_chunks, SSD_UNROLL)),
        grid=(b, s // t_blk),
        in_specs=[pl.BlockSpec((1, t_blk, d), lambda bi, si: (bi, si, 0)),
                  full((d, SSD_COLS)),
                  full((SSD_CONV, SSD_XBC_P)), full((1, SSD_XBC_P)),
                  full((1, LANES)), full((1, LANES)),
                  full((1, D_SSD_P)), full((1, D_SSD_P)),
                  full((LANES, D_SSD_P)), full((sub, sub)), full((sub, 2 * sub))],
        out_specs=pl.BlockSpec((1, t_blk, D_SSD_P), lambda bi, si: (bi, si, 0)),
        out_shape=jax.ShapeDtypeStruct((b, s, D_SSD_P), BF16),
        scratch_shapes=[pltpu.VMEM((t_blk + CONV_HALO, SSD_XBC_P), F32),
                        pltpu.VMEM((t_blk, SSD_XBC_P), F32),
                        pltpu.VMEM((t_blk, D_SSD_P), F32),
                        pltpu.VMEM((t_blk, D_SSD_P), F32),
                        pltpu.VMEM((t_blk, D_SSD_P), F32),
                        pltpu.VMEM((n_chunks, LANES, LANES), F32),
                        pltpu.VMEM((SSD_GROUPS, SSD_STATE, SSD_GWP), F32)],
        compiler_params=pltpu.CompilerParams(
            dimension_semantics=("parallel", "arbitrary"),
            vmem_limit_bytes=VMEM_LIMIT),
        name="ssd_mixer",
    )(u, w_ssd, conv_w, conv_b, dt_bias, alog_c, dskip_e, nw_e, expand, btril, dup)


def _out_kernel(h_ref, ohg_ref, opool_ref, ossd_ref, p_ref, w1_ref, w2_ref, w3_ref, wpg_ref, wpe_ref,
                nw_ref, *out_refs, emit_h):
    dot = lambda a, w: jnp.dot(a, w, preferred_element_type=F32)
    acc = h_ref[...] + dot(ohg_ref[...], w1_ref[...]) + dot(opool_ref[...], w2_ref[...]) \
        + dot(ossd_ref[...], w3_ref[...])
    gate = _sigmoid(_dot(acc, wpg_ref[...]))
    hn = acc + gate * _dot(p_ref[...], wpe_ref[...])
    y = hn * lax.rsqrt(jnp.mean(hn * hn, axis=-1, keepdims=True) + EPS) * nw_ref[...]
    if emit_h:
        out_refs[0][...] = hn
        out_refs[1][...] = y.astype(out_refs[1].dtype)
    else:
        out_refs[0][...] = y


def _out_call(h2d, ohg, opool, ossd, p2d, w1, w2, w3, wpg, wpe, nw, tm, emit_h):
    n, d = h2d.shape
    rows = lambda width: pl.BlockSpec((tm, width), lambda i: (i, 0))
    full = lambda a: pl.BlockSpec(a.shape, lambda i: (0, 0), pipeline_mode=pl.Buffered(1))
    if emit_h:
        out_shape = (jax.ShapeDtypeStruct((n, d), F32), jax.ShapeDtypeStruct((n, d), BF16))
        out_specs = (rows(d), rows(d))
    else:
        out_shape = jax.ShapeDtypeStruct((n, d), F32)
        out_specs = rows(d)
    return pl.pallas_call(
        functools.partial(_out_kernel, emit_h=emit_h),
        grid=(n // tm,),
        in_specs=[rows(d), rows(D_HG), rows(D_POOL), rows(D_SSD_P), rows(P_DIM),
                  full(w1), full(w2), full(w3), full(wpg), full(wpe), pl.BlockSpec((1, d), lambda i: (0, 0))],
        out_specs=out_specs,
        out_shape=out_shape,
        compiler_params=pltpu.CompilerParams(
            dimension_semantics=("parallel",), vmem_limit_bytes=VMEM_LIMIT),
        name="out_proj",
    )(h2d, ohg, opool, ossd, p2d, w1, w2, w3, wpg, wpe, nw.reshape(1, d))


def _pad_groups(a):
    lead = a.shape[:-1]
    a = a.reshape(lead + (SSD_GROUPS, SSD_GW))
    a = jnp.pad(a, [(0, 0)] * len(lead) + [(0, 0), (0, SSD_GWP - SSD_GW)])
    return a.reshape(lead + (D_SSD_P,))


def _pad_lanes(a):
    return jnp.pad(a, [(0, 0)] * (a.ndim - 1) + [(0, LANES - a.shape[-1])])


C_POOL = 4 * D_HG
C_XBC = C_POOL + 2 * D_POOL
C_B = C_XBC + D_SSD
C_C = C_B + SSD_GROUPS * SSD_STATE
C_DT = C_C + SSD_GROUPS * SSD_STATE
C_Z = C_DT + SSD_HEADS
N_IN = C_Z + D_SSD
N_IN_MAIN = (N_IN // LANES) * LANES
WPREP_ROWS = 256


def _wprep_kernel(w_ref, tail_ref, hg_ref, pool_ref, ssd_ref):
    rows = w_ref.shape[1]
    lane = lax.broadcasted_iota(jnp.int32, (rows, LANES), 1)
    tail = jnp.where(lane < N_IN - N_IN_MAIN, tail_ref[0], 0.0)
    hg_ref[0] = w_ref[0, :, 0:C_POOL].astype(BF16)
    pool_ref[0] = w_ref[0, :, C_POOL:C_XBC].astype(BF16)

    def window(a0, width):
        if a0 + width <= N_IN_MAIN:
            return w_ref[0, :, a0:a0 + width]
        assert a0 + width == N_IN_MAIN + LANES
        return jnp.concatenate([w_ref[0, :, a0:N_IN_MAIN], tail], axis=1)

    lane_g = lax.broadcasted_iota(jnp.int32, (rows, SSD_GWP), 1)

    def group(a):
        a0 = (a // LANES) * LANES
        win = window(a0, SSD_GWP + LANES)
        if a != a0:
            win = pltpu.roll(win, SSD_GWP + LANES - (a - a0), 1)
        return jnp.where(lane_g < SSD_GW, win[:, :SSD_GWP], 0.0)

    for g in range(SSD_GROUPS):
        ssd_ref[0, :, g * SSD_GWP:(g + 1) * SSD_GWP] = group(C_XBC + g * SSD_GW).astype(BF16)
        zo = SSD_XBC_P + g * SSD_GWP
        ssd_ref[0, :, zo:zo + SSD_GWP] = group(C_Z + g * SSD_GW).astype(BF16)
    ssd_ref[0, :, D_SSD_P:SSD_XBC_P] = w_ref[0, :, C_B:C_DT].astype(BF16)
    dt = jnp.where(lane < SSD_HEADS, window(C_DT, LANES), 0.0)
    ssd_ref[0, :, SSD_XBC_P + D_SSD_P:] = dt.astype(BF16)


def _wprep_call(w_in):
    depth, d, n_in = w_in.shape
    assert n_in == N_IN and C_DT % LANES == 0 and d % WPREP_ROWS == 0
    out = lambda width: pl.BlockSpec((1, WPREP_ROWS, width), lambda i, r: (i, r, 0))
    return pl.pallas_call(
        _wprep_kernel,
        grid=(depth, d // WPREP_ROWS),
        in_specs=[pl.BlockSpec((1, WPREP_ROWS, N_IN_MAIN), lambda i, r: (i, r, 0)),
                  pl.BlockSpec((1, WPREP_ROWS, LANES), lambda i, r: (i, r, N_IN_MAIN // LANES))],
        out_specs=(out(C_POOL), out(C_XBC - C_POOL), out(SSD_COLS)),
        out_shape=(jax.ShapeDtypeStruct((depth, d, C_POOL), BF16),
                   jax.ShapeDtypeStruct((depth, d, C_XBC - C_POOL), BF16),
                   jax.ShapeDtypeStruct((depth, d, SSD_COLS), BF16)),
        compiler_params=pltpu.CompilerParams(
            dimension_semantics=("parallel", "parallel"), vmem_limit_bytes=VMEM_LIMIT),
        name="weight_prep",
    )(w_in, w_in)


def kernel(x, p, norm_w, w_in, hg_lb, hg_norm_w, pool_w, pool_scale, conv_w, conv_b, dt_bias, a_log, d_skip,
           ssd_norm_w, w_out, w_pe, w_pg, final_norm_w):
    b, s, d = x.shape
    depth = w_in.shape[0]
    n = b * s
    t_hg = min(1024, s)
    t_pool = min(1024, s)
    t_ssd = min(512, s)
    tm = min(1024, n)

    lb_all = jnp.cumsum(jax.nn.softmax(hg_lb.astype(F32), axis=0), axis=0)
    lb_all = lb_all - lb_all[0]

    o_hg0, o_pool0 = 0, D_HG
    o_ssd0 = D_HG + D_POOL

    w_hg_all, w_pool_all, w_ssd_all = _wprep_call(w_in)

    h = x.reshape(n, d)
    u = _rms_norm_call(h, norm_w[0], tm).reshape(b, s, d)
    out = None
    for i in range(depth):
        lb = lb_all[i].reshape(1, D_HG)
        o_hg = _hgrn_call(u, w_hg_all[i], jnp.log(lb) * LOG2E, jnp.log1p(-lb) * LOG2E,
                          hg_norm_w[i].reshape(1, D_HG), t_hg)
        o_pool = _pool_call(u, w_pool_all[i], pool_w[i].astype(BF16),
                            pool_scale[i].reshape(1, D_POOL), t_pool)
        xbc_split = lambda a: jnp.concatenate([_pad_groups(a[..., :D_SSD]), a[..., D_SSD:]], axis=-1)
        rep = lambda a: _pad_groups(jnp.repeat(a, SSD_HEAD_DIM)).reshape(1, D_SSD_P)
        o_ssd = _ssd_call(u, w_ssd_all[i], xbc_split(conv_w[i]), xbc_split(conv_b[i]).reshape(1, SSD_XBC_P),
                          _pad_lanes(dt_bias[i]).reshape(1, LANES), _pad_lanes(a_log[i]).reshape(1, LANES),
                          rep(d_skip[i]), _pad_groups(ssd_norm_w[i]).reshape(1, D_SSD_P), t_ssd)
        wo = w_out[i]
        w3 = jnp.pad(wo[o_ssd0:].reshape(SSD_GROUPS, SSD_GW, d), ((0, 0), (0, SSD_GWP - SSD_GW), (0, 0)))
        last = i == depth - 1
        nw_next = final_norm_w if last else norm_w[i + 1]
        res = _out_call(h, o_hg.reshape(n, D_HG), o_pool.reshape(n, D_POOL), o_ssd.reshape(n, D_SSD_P),
                        p[i].reshape(n, P_DIM), wo[o_hg0:o_pool0].astype(BF16), wo[o_pool0:o_ssd0].astype(BF16),
                        w3.reshape(D_SSD_P, d).astype(BF16), w_pg[i].astype(BF16), w_pe[i].astype(BF16),
                        nw_next, tm, emit_h=not last)
        if last:
            out = res
        else:
            h, u = res
            u = u.reshape(b, s, d)
    return out.reshape(b, s, d)
```

```python
import functools
import math

import jax
import jax.numpy as jnp
from jax import lax
from jax.experimental import pallas as pl
from jax.experimental.pallas import tpu as pltpu

F32 = jnp.float32
BF16 = jnp.bfloat16

CHUNK = 64
EPS = 1e-6
D_MODEL = 1024
P_DIM = 256
HG_HEADS = 6
HG_D = 128
HG_PAIR = 2
HG_LEVELS = (64, 32, 16, 8, 4)
HG_HEAD_ROWS = 256
HG_GROUP = 16
HG_UNROLL = 32
LOG2E = 1.4426950408889634
D_HG = HG_HEADS * HG_D
POOL_WINDOWS = (2, 4, 8, 16)
POOL_CH = 128
D_POOL = len(POOL_WINDOWS) * POOL_CH
POOL_HALO = 16
SSD_HEADS = 12
SSD_HEAD_DIM = 64
SSD_GROUPS = 4
SSD_HPG = SSD_HEADS // SSD_GROUPS
SSD_GW = SSD_HPG * SSD_HEAD_DIM
SSD_GWP = 256
D_SSD = SSD_HEADS * SSD_HEAD_DIM
D_SSD_P = SSD_GROUPS * SSD_GWP
SSD_STATE = 128
SSD_CONV = 4
CONV_HALO = 8
SSD_SEL_ROWS = 256
SSD_UNROLL = 8
LANES = 128
SSD_XBC_P = D_SSD_P + 2 * SSD_GROUPS * SSD_STATE
SSD_COLS = SSD_XBC_P + D_SSD_P + LANES
VMEM_LIMIT = 48 * 1024 * 1024


def _dot(a, b):
    return jnp.dot(a.astype(BF16), b.astype(BF16), preferred_element_type=F32)


def _dot_nt(a, b):
    return lax.dot_general(a.astype(BF16), b.astype(BF16), (((1,), (1,)), ((), ())),
                           preferred_element_type=F32)


def _dot_tn(a, b):
    return lax.dot_general(a.astype(BF16), b.astype(BF16), (((0,), (0,)), ((), ())),
                           preferred_element_type=F32)


def _split3(x):
    hi = x.astype(BF16)
    r1 = x - hi.astype(F32)
    mid = r1.astype(BF16)
    lo = (r1 - mid.astype(F32)).astype(BF16)
    return hi, mid, lo


def _sel_left(m, x):
    hi, mid, lo = _split3(x)
    d = lambda p: jnp.dot(m, p, preferred_element_type=F32)
    return d(hi) + (d(mid) + d(lo))


def _sel_left3(m3, x):
    return jnp.dot(m3, jnp.concatenate(_split3(x), axis=0), preferred_element_type=F32)


def _sel_right3(x, m3):
    return jnp.dot(jnp.concatenate(_split3(x), axis=1), m3, preferred_element_type=F32)


def _chunk_rows(i):
    if isinstance(i, int):
        return pl.ds(i * CHUNK, CHUNK)
    return pl.ds(pl.multiple_of(i * CHUNK, CHUNK), CHUNK)


def _run_trips(n_trips, body):
    if n_trips == 1:
        body(0, 0)
    else:
        lax.fori_loop(0, n_trips, body, 0)


def _sigmoid(x):
    return 1.0 / (1.0 + jnp.exp(-x))


def _silu(x):
    return x * _sigmoid(x)


def _softplus(x):
    return jnp.maximum(x, 0.0) + jnp.log1p(jnp.exp(-jnp.abs(x)))


def _norm_kernel(x_ref, w_ref, o_ref):
    x = x_ref[...]
    y = x * lax.rsqrt(jnp.mean(x * x, axis=-1, keepdims=True) + EPS)
    o_ref[...] = (y * w_ref[...]).astype(o_ref.dtype)


def _rms_norm_call(x2d, w, tm):
    n, d = x2d.shape
    return pl.pallas_call(
        _norm_kernel,
        grid=(n // tm,),
        in_specs=[pl.BlockSpec((tm, d), lambda i: (i, 0)),
                  pl.BlockSpec((1, d), lambda i: (0, 0))],
        out_specs=pl.BlockSpec((tm, d), lambda i: (i, 0)),
        out_shape=jax.ShapeDtypeStruct((n, d), BF16),
        compiler_params=pltpu.CompilerParams(dimension_semantics=("parallel",)),
        name="rms_norm_in",
    )(x2d, w.reshape(1, d))


def _hgrn_kernel(u_ref, wq_ref, wf_ref, wv_ref, wg_ref, loglb_ref, log1mlb_ref, nw_ref, o_ref,
                 proj_ref, vg_ref, cum_ref, pair_ref, state_ref, *, n_chunks, unroll):
    @pl.when(pl.program_id(2) == 0)
    def _():
        state_ref[...] = jnp.zeros_like(state_ref)

    pw = HG_PAIR * HG_D
    w_refs = (wq_ref, wf_ref, wv_ref, wg_ref)

    def project(kind, r0, r1):
        dst, slot = (proj_ref, kind) if kind < 2 else (vg_ref, kind - 2)
        dst[r0:r1, slot * pw:(slot + 1) * pw] = jnp.dot(u_ref[0, r0:r1, :], w_refs[kind][...],
                                                        preferred_element_type=F32)

    t_blk = n_chunks * CHUNK
    for r0 in range(0, t_blk, HG_HEAD_ROWS):
        project(1, r0, min(r0 + HG_HEAD_ROWS, t_blk))
    for kind in (0, 2, 3):
        project(kind, 0, t_blk)

    row = lax.broadcasted_iota(jnp.int32, (CHUNK, HG_D), 0)
    sub = lax.broadcasted_iota(jnp.int32, (CHUNK // 8, 8, HG_D), 1)
    r64 = lax.broadcasted_iota(jnp.int32, (CHUNK, CHUNK), 0)
    c64 = lax.broadcasted_iota(jnp.int32, (CHUNK, CHUNK), 1)
    r3 = lax.broadcasted_iota(jnp.int32, (CHUNK, 3 * CHUNK), 0)
    c3 = lax.broadcasted_iota(jnp.int32, (CHUNK, 3 * CHUNK), 1)
    tril3 = (r3 >= (c3 % CHUNK)).astype(BF16)
    level_mask = {bs: ((r64 // bs) == (c64 // bs)) & ((r64 & (bs // 2)) != 0) & ((c64 & (bs // 2)) == 0)
                  for bs in HG_LEVELS}
    level_sign = {bs: jnp.where((row & (bs // 2)) != 0, 1.0, -1.0) for bs in HG_LEVELS}
    diag0 = r64 == c64
    diag1 = (r64 == c64 + 1) & ((r64 & 1) == 1)
    odd = (row & 1) == 1

    def ref_rows(cum, bs):
        half = bs // 2
        if bs >= 8:
            c3 = cum.reshape(CHUNK // bs, bs, HG_D)
            return jnp.broadcast_to(c3[:, half - 1:half, :], c3.shape).reshape(CHUNK, HG_D)
        c3 = cum.reshape(CHUNK // 8, 8, HG_D)
        lo = jnp.broadcast_to(c3[:, 1:2, :], c3.shape)
        hi = jnp.broadcast_to(c3[:, 5:6, :], c3.shape)
        return jnp.where(sub < 4, lo, hi).reshape(CHUNK, HG_D)

    def gates(c, carry):
        rows = _chunk_rows(c)
        for hh in range(HG_PAIR):
            lanes = slice(hh * HG_D, (hh + 1) * HG_D)
            z = proj_ref[rows, pw + hh * HG_D:pw + (hh + 1) * HG_D]
            loglb = loglb_ref[:, lanes]
            log1mlb = log1mlb_ref[:, lanes]
            zs = z * LOG2E
            log_sig = jnp.minimum(zs, 0.0) - jnp.log2(1.0 + jnp.exp2(-jnp.abs(zs)))
            b_ = log1mlb + log_sig
            log_f = jnp.maximum(loglb, b_) + jnp.log2(1.0 + jnp.exp2(-jnp.abs(loglb - b_)))
            f = jnp.exp2(log_f)
            kk = 1.0 - f
            proj_ref[rows, pw + hh * HG_D:pw + (hh + 1) * HG_D] = kk
            cum_ref[rows, lanes] = _sel_left3(tril3, log_f)
            pair_ref[rows, lanes] = jnp.where(odd, pltpu.roll(kk, 1, 0) * f, 0.0)
        return carry

    _run_trips(n_chunks // unroll, lambda c, carry: [gates(c * unroll + j, carry) for j in range(unroll)][-1])

    def trip(c, carry):
        for j0 in range(0, unroll, min(unroll, HG_GROUP)):
            stage_group(c, range(j0, j0 + min(unroll, HG_GROUP)))
        return carry

    def stage_group(c, chunk_ids):
        chains = [(j, hh) for j in chunk_ids for hh in range(HG_PAIR)]
        rows = {j: _chunk_rows(c * unroll + j) for j in chunk_ids}
        lanes = [slice(hh * HG_D, (hh + 1) * HG_D) for hh in range(HG_PAIR)]
        col = lambda kind, hh: slice(kind * pw + hh * HG_D, kind * pw + (hh + 1) * HG_D)
        q = {ch: proj_ref[rows[ch[0]], col(0, ch[1])] for ch in chains}
        kk = {ch: proj_ref[rows[ch[0]], col(1, ch[1])] for ch in chains}
        cum = {ch: cum_ref[rows[ch[0]], lanes[ch[1]]] for ch in chains}

        scores = {}
        for ch in chains:
            acc = None
            for bs in HG_LEVELS:
                e = jnp.exp2((cum[ch] - ref_rows(cum[ch], bs)) * level_sign[bs])
                m = jnp.where(level_mask[bs], _dot_nt(q[ch] * e, kk[ch] * e), 0.0)
                acc = m if acc is None else acc + m
            rs0 = jnp.sum(q[ch] * kk[ch], axis=1, keepdims=True)
            rs1 = jnp.sum(q[ch] * pair_ref[rows[ch[0]], lanes[ch[1]]], axis=1, keepdims=True)
            scores[ch] = acc + jnp.where(diag0, rs0, 0.0) + jnp.where(diag1, rs1, 0.0)

        inter = {}
        for ch in chains:
            j, hh = ch
            last = cum[ch][CHUNK - 1:CHUNK, :]
            st = state_ref[hh]
            inter[ch] = _dot_nt(q[ch] * jnp.exp2(cum[ch]), st)
            v = vg_ref[rows[j], col(0, hh)]
            state_ref[hh] = st * jnp.exp2(last) + _dot_tn(v, kk[ch] * jnp.exp2(last - cum[ch]))

        for ch in chains:
            j, hh = ch
            o = inter[ch] + _dot(scores[ch], vg_ref[rows[j], col(0, hh)])
            o = o * lax.rsqrt(jnp.mean(o * o, axis=-1, keepdims=True) + EPS) * nw_ref[:, lanes[hh]]
            g = vg_ref[rows[j], col(1, hh)]
            o_ref[0, rows[j], lanes[hh]] = (o * _silu(g)).astype(o_ref.dtype)

    _run_trips(n_chunks // unroll, trip)


def _hgrn_call(u, w_hg, loglb, log1mlb, nw, t_blk):
    b, s, d = u.shape
    n_pairs = HG_HEADS // HG_PAIR
    pw = HG_PAIR * HG_D
    grid = (b, n_pairs, s // t_blk)
    vec = pl.BlockSpec((1, pw), lambda bi, hp, si: (0, hp))
    wspec = lambda kind: pl.BlockSpec((d, pw), lambda bi, hp, si: (0, kind * n_pairs + hp))
    return pl.pallas_call(
        functools.partial(_hgrn_kernel, n_chunks=t_blk // CHUNK, unroll=math.gcd(t_blk // CHUNK, HG_UNROLL)),
        grid=grid,
        in_specs=[pl.BlockSpec((1, t_blk, d), lambda bi, hp, si: (bi, si, 0)),
                  wspec(0), wspec(1), wspec(2), wspec(3),
                  vec, vec, vec],
        out_specs=pl.BlockSpec((1, t_blk, pw), lambda bi, hp, si: (bi, si, hp)),
        out_shape=jax.ShapeDtypeStruct((b, s, D_HG), BF16),
        scratch_shapes=[pltpu.VMEM((t_blk, 2 * pw), F32),
                        pltpu.VMEM((t_blk, 2 * pw), F32),
                        pltpu.VMEM((t_blk, pw), F32),
                        pltpu.VMEM((t_blk, pw), F32),
                        pltpu.VMEM((HG_PAIR, HG_D, HG_D), F32)],
        compiler_params=pltpu.CompilerParams(
            dimension_semantics=("parallel", "parallel", "arbitrary"),
            vmem_limit_bytes=VMEM_LIMIT),
        name="hgrn2_mixer",
    )(u, w_hg, w_hg, w_hg, w_hg, loglb, log1mlb, nw)


def _pool_kernel(u_ref, w_ref, pw_ref, ps_ref, o_ref, ext_ref, *, t_blk):
    si = pl.program_id(1)

    @pl.when(si == 0)
    def _():
        _pool_halo(ext_ref, t_blk, first=True)

    @pl.when(si > 0)
    def _():
        _pool_halo(ext_ref, t_blk, first=False)

    _pool_block(u_ref, w_ref, pw_ref, ps_ref, o_ref, ext_ref, si, t_blk)


def _pool_halo(ext_ref, t_blk, first):
    if first:
        ext_ref[0:POOL_HALO, :] = jnp.zeros((POOL_HALO, D_POOL), F32)
    else:
        ext_ref[0:POOL_HALO, :] = ext_ref[t_blk:t_blk + POOL_HALO, :]


def _pool_block(u_ref, w_ref, pw_ref, ps_ref, o_ref, ext_ref, si, t_blk):
    proj = jnp.dot(u_ref[0], w_ref[...], preferred_element_type=F32)
    ext_ref[POOL_HALO:POOL_HALO + t_blk, :] = proj[:, :D_POOL]

    pos = (si * t_blk + 1 + lax.broadcasted_iota(jnp.int32, (t_blk, POOL_CH), 0)).astype(F32)
    for gi, win in enumerate(POOL_WINDOWS):
        lo = gi * POOL_CH
        acc = ext_ref[:, lo:lo + POOL_CH]
        shift = 1
        while shift < win:
            acc = acc + pltpu.roll(acc, shift, 0)
            shift *= 2
        cur = proj[:, lo:lo + POOL_CH]
        pooled = acc[POOL_HALO:, :] / jnp.minimum(pos, float(win)) - cur
        y = _dot(pooled, pw_ref[gi]) * ps_ref[:, lo:lo + POOL_CH]
        gate = proj[:, D_POOL + lo:D_POOL + lo + POOL_CH]
        o_ref[0, :, lo:lo + POOL_CH] = (y * _silu(gate)).astype(o_ref.dtype)


def _pool_call(u, w_pool, pool_w, pool_scale, t_blk):
    b, s, d = u.shape
    return pl.pallas_call(
        functools.partial(_pool_kernel, t_blk=t_blk),
        grid=(b, s // t_blk),
        in_specs=[pl.BlockSpec((1, t_blk, d), lambda bi, si: (bi, si, 0)),
                  pl.BlockSpec((d, 2 * D_POOL), lambda bi, si: (0, 0)),
                  pl.BlockSpec((len(POOL_WINDOWS), POOL_CH, POOL_CH), lambda bi, si: (0, 0, 0)),
                  pl.BlockSpec((1, D_POOL), lambda bi, si: (0, 0))],
        out_specs=pl.BlockSpec((1, t_blk, D_POOL), lambda bi, si: (bi, si, 0)),
        out_shape=jax.ShapeDtypeStruct((b, s, D_POOL), BF16),
        scratch_shapes=[pltpu.VMEM((t_blk + POOL_HALO, D_POOL), F32)],
        compiler_params=pltpu.CompilerParams(
            dimension_semantics=("parallel", "arbitrary"),
            vmem_limit_bytes=VMEM_LIMIT),
        name="pool_mixer",
    )(u, w_pool, pool_w, pool_scale)


def _ssd_kernel(u_ref, w_ref, cw_ref, cb_ref, dtb_ref, alogc_ref, dskip_ref, nw_ref,
                exp_ref, btril_ref, dup_ref, o_ref,
                ext_ref, act_ref, z_ref, xdt_ref, cume_ref, cumt_ref, state_ref, *, t_blk, unroll):
    si = pl.program_id(1)
    n_chunks = t_blk // CHUNK

    @pl.when(si == 0)
    def _():
        ext_ref[0:CONV_HALO, :] = jnp.zeros((CONV_HALO, SSD_XBC_P), F32)
        state_ref[...] = jnp.zeros_like(state_ref)

    @pl.when(si > 0)
    def _():
        ext_ref[0:CONV_HALO, :] = ext_ref[t_blk:t_blk + CONV_HALO, :]

    proj = jnp.dot(u_ref[0], w_ref[...], preferred_element_type=F32)
    ext_ref[CONV_HALO:CONV_HALO + t_blk, :] = proj[:, :SSD_XBC_P]
    z_ref[...] = proj[:, SSD_XBC_P:SSD_XBC_P + D_SSD_P]

    assert SSD_CONV == 4
    e = ext_ref[...]
    e2 = pltpu.roll(e, 2, 0)
    even = cw_ref[3:4, :] * e + cw_ref[1:2, :] * e2
    odd = cw_ref[2:3, :] * e + cw_ref[0:1, :] * e2
    xc = (even + pltpu.roll(odd, 1, 0))[CONV_HALO:, :] + cb_ref[...]
    xc = _silu(xc)
    act_ref[...] = xc

    dt = _softplus(proj[:, SSD_XBC_P + D_SSD_P:] + dtb_ref[...])
    exp3 = jnp.concatenate([exp_ref[...]] * 3, axis=0)
    xdt_ref[...] = xc[:, :D_SSD_P] * _sel_right3(dt, exp3)

    dta = dt * (-jnp.exp(alogc_ref[...]) * LOG2E)
    sub = btril_ref.shape[0]
    dup3 = jnp.concatenate([dup_ref[...]] * 3, axis=0)
    for r0 in range(0, t_blk, sub):
        cum_c = _sel_left(btril_ref[...], dta[r0:r0 + sub, :])
        hi, mid, lo = _split3(cum_c)
        cume_ref[r0:r0 + sub, :] = jnp.dot(jnp.concatenate([hi, mid, lo], axis=1), exp3,
                                           preferred_element_type=F32)
        cum_t = lax.dot_general(jnp.concatenate([hi, mid, lo], axis=0), dup3,
                                (((0,), (0,)), ((), ())), preferred_element_type=F32)
        for c in range(sub // CHUNK):
            cumt_ref[r0 // CHUNK + c] = cum_t[:, c * LANES:(c + 1) * LANES]

    r64 = lax.broadcasted_iota(jnp.int32, (CHUNK, CHUNK), 0)
    c64 = lax.broadcasted_iota(jnp.int32, (CHUNK, CHUNK), 1)
    causal = r64 >= c64
    r128 = lax.broadcasted_iota(jnp.int32, (CHUNK, LANES), 0)
    l128 = lax.broadcasted_iota(jnp.int32, (CHUNK, LANES), 1)
    causal2 = r128 >= (l128 & (CHUNK - 1))
    first_half = lax.broadcasted_iota(jnp.int32, (1, LANES), 1) < CHUNK
    rb = lax.broadcasted_iota(jnp.int32, (LANES, LANES), 0)
    lb = lax.broadcasted_iota(jnp.int32, (LANES, LANES), 1)
    blockdiag = (rb >= CHUNK) == (lb >= CHUNK)
    b_off = D_SSD_P
    c_off = D_SSD_P + SSD_GROUPS * SSD_STATE
    groups = range(SSD_GROUPS)

    gl = [slice(g * SSD_GWP, (g + 1) * SSD_GWP) for g in groups]
    bl = [slice(b_off + g * SSD_STATE, b_off + (g + 1) * SSD_STATE) for g in groups]
    cl = [slice(c_off + g * SSD_STATE, c_off + (g + 1) * SSD_STATE) for g in groups]

    def trip(c, carry):
        chains = [(j, g) for j in range(unroll) for g in groups]
        rows = [_chunk_rows(c * unroll + j) for j in range(unroll)]
        cum_t = [cumt_ref[c * unroll + j] for j in range(unroll)]

        cb2 = {}
        for j, g in chains:
            bm = act_ref[rows[j], bl[g]]
            cb2[j, g] = _dot_nt(act_ref[rows[j], cl[g]], jnp.concatenate([bm, bm], axis=0))
        y_diag = {}
        for j, g in chains:
            hd = g * SSD_HPG
            cs01 = jnp.where(first_half, cum_t[j][hd:hd + 1, :], cum_t[j][hd + 1:hd + 2, :])
            ct01 = cume_ref[rows[j], g * SSD_GWP:g * SSD_GWP + LANES]
            seg01 = jnp.exp2(jnp.where(causal2, ct01 - cs01, -jnp.inf))
            ct2 = cume_ref[rows[j], g * SSD_GWP + LANES:g * SSD_GWP + LANES + CHUNK]
            seg2 = jnp.exp2(jnp.where(causal, ct2 - cum_t[j][hd + 2:hd + 3, 0:CHUNK], -jnp.inf))
            x01 = xdt_ref[rows[j], g * SSD_GWP:g * SSD_GWP + LANES]
            xbd = jnp.where(blockdiag, jnp.concatenate([x01, x01], axis=0), 0.0)
            y01 = _dot(cb2[j, g] * seg01, xbd)
            y2 = _dot(cb2[j, g][:, :CHUNK] * seg2, xdt_ref[rows[j], g * SSD_GWP + LANES:(g + 1) * SSD_GWP])
            y_diag[j, g] = jnp.concatenate([y01, y2], axis=1)

        y_off = {}
        for j, g in chains:
            cum_e = cume_ref[rows[j], gl[g]]
            last_e = cum_e[CHUNK - 1:CHUNK, :]
            hst = state_ref[g]
            y_off[j, g] = _dot(act_ref[rows[j], cl[g]], hst) * jnp.exp2(cum_e)
            xdec = xdt_ref[rows[j], gl[g]] * jnp.exp2(last_e - cum_e)
            state_ref[g] = hst * jnp.exp2(last_e) + _dot_tn(act_ref[rows[j], bl[g]], xdec)

        for j, g in chains:
            y = y_diag[j, g] + y_off[j, g] + act_ref[rows[j], gl[g]] * dskip_ref[:, gl[g]]
            y = y * _silu(z_ref[rows[j], gl[g]])
            ms = jnp.sum(y * y, axis=-1, keepdims=True) * (1.0 / SSD_GW)
            y = y * lax.rsqrt(ms + EPS) * nw_ref[:, gl[g]]
            o_ref[0, rows[j], gl[g]] = y.astype(o_ref.dtype)
        return carry

    _run_trips(n_chunks // unroll, trip)


def _ssd_call(u, w_ssd, conv_w, conv_b, dt_bias, alog_c, dskip_e, nw_e, t_blk):
    b, s, d = u.shape
    n_chunks = t_blk // CHUNK
    heads = jnp.arange(SSD_HEADS)
    lanes_e = (heads // SSD_HPG) * SSD_GWP + (heads % SSD_HPG) * SSD_HEAD_DIM
    col_e = jnp.arange(D_SSD_P)[None, :]
    expand = ((col_e >= lanes_e[:, None]) & (col_e < lanes_e[:, None] + SSD_HEAD_DIM))
    expand = jnp.pad(expand, ((0, LANES - SSD_HEADS), (0, 0))).astype(BF16)
    sub = math.gcd(t_blk, SSD_SEL_ROWS)
    ti = jnp.arange(sub)
    btril = ((ti[:, None] >= ti[None, :]) & (ti[:, None] // CHUNK == ti[None, :] // CHUNK)).astype(BF16)
    cj = jnp.arange(2 * sub)
    dup = (ti[:, None] == (cj[None, :] // LANES) * CHUNK + cj[None, :] % CHUNK).astype(BF16)

    full = lambda shape: pl.BlockSpec(shape, lambda bi, si: (0,) * len(shape))
    return pl.pallas_call(
        functools.partial(_ssd_kernel, t_blk=t_blk, unroll=math.gcd(n_chunks, SSD_UNROLL)),
        grid=(b, s // t_blk),
        in_specs=[pl.BlockSpec((1, t_blk, d), lambda bi, si: (bi, si, 0)),
                  full((d, SSD_COLS)),
                  full((SSD_CONV, SSD_XBC_P)), full((1, SSD_XBC_P)),
                  full((1, LANES)), full((1, LANES)),
                  full((1, D_SSD_P)), full((1, D_SSD_P)),
                  full((LANES, D_SSD_P)), full((sub, sub)), full((sub, 2 * sub))],
        out_specs=pl.BlockSpec((1, t_blk, D_SSD_P), lambda bi, si: (bi, si, 0)),
        out_shape=jax.ShapeDtypeStruct((b, s, D_SSD_P), BF16),
        scratch_shapes=[pltpu.VMEM((t_blk + CONV_HALO, SSD_XBC_P), F32),
                        pltpu.VMEM((t_blk, SSD_XBC_P), F32),
                        pltpu.VMEM((t_blk, D_SSD_P), F32),
                        pltpu.VMEM((t_blk, D_SSD_P), F32),
                        pltpu.VMEM((t_blk, D_SSD_P), F32),
                        pltpu.VMEM((n_chunks, LANES, LANES), F32),
                        pltpu.VMEM((SSD_GROUPS, SSD_STATE, SSD_GWP), F32)],
        compiler_params=pltpu.CompilerParams(
            dimension_semantics=("parallel", "arbitrary"),
            vmem_limit_bytes=VMEM_LIMIT),
        name="ssd_mixer",
    )(u, w_ssd, conv_w, conv_b, dt_bias, alog_c, dskip_e, nw_e, expand, btril, dup)


def _out_kernel(h_ref, ohg_ref, opool_ref, ossd_ref, p_ref, w1_ref, w2_ref, w3_ref, wpg_ref, wpe_ref,
                nw_ref, *out_refs, emit_h):
    dot = lambda a, w: jnp.dot(a, w, preferred_element_type=F32)
    acc = h_ref[...] + dot(ohg_ref[...], w1_ref[...]) + dot(opool_ref[...], w2_ref[...]) \
        + dot(ossd_ref[...], w3_ref[...])
    gate = _sigmoid(_dot(acc, wpg_ref[...]))
    hn = acc + gate * _dot(p_ref[...], wpe_ref[...])
    y = hn * lax.rsqrt(jnp.mean(hn * hn, axis=-1, keepdims=True) + EPS) * nw_ref[...]
    if emit_h:
        out_refs[0][...] = hn
        out_refs[1][...] = y.astype(out_refs[1].dtype)
    else:
        out_refs[0][...] = y


def _out_call(h2d, ohg, opool, ossd, p2d, w1, w2, w3, wpg, wpe, nw, tm, emit_h):
    n, d = h2d.shape
    rows = lambda width: pl.BlockSpec((tm, width), lambda i: (i, 0))
    full = lambda a: pl.BlockSpec(a.shape, lambda i: (0, 0), pipeline_mode=pl.Buffered(1))
    if emit_h:
        out_shape = (jax.ShapeDtypeStruct((n, d), F32), jax.ShapeDtypeStruct((n, d), BF16))
        out_specs = (rows(d), rows(d))
    else:
        out_shape = jax.ShapeDtypeStruct((n, d), F32)
        out_specs = rows(d)
    return pl.pallas_call(
        functools.partial(_out_kernel, emit_h=emit_h),
        grid=(n // tm,),
        in_specs=[rows(d), rows(D_HG), rows(D_POOL), rows(D_SSD_P), rows(P_DIM),
                  full(w1), full(w2), full(w3), full(wpg), full(wpe), pl.BlockSpec((1, d), lambda i: (0, 0))],
        out_specs=out_specs,
        out_shape=out_shape,
        compiler_params=pltpu.CompilerParams(
            dimension_semantics=("parallel",), vmem_limit_bytes=VMEM_LIMIT),
        name="out_proj",
    )(h2d, ohg, opool, ossd, p2d, w1, w2, w3, wpg, wpe, nw.reshape(1, d))


def _pad_groups(a):
    lead = a.shape[:-1]
    a = a.reshape(lead + (SSD_GROUPS, SSD_GW))
    a = jnp.pad(a, [(0, 0)] * len(lead) + [(0, 0), (0, SSD_GWP - SSD_GW)])
    return a.reshape(lead + (D_SSD_P,))


def _pad_lanes(a):
    return jnp.pad(a, [(0, 0)] * (a.ndim - 1) + [(0, LANES - a.shape[-1])])


C_POOL = 4 * D_HG
C_XBC = C_POOL + 2 * D_POOL
C_B = C_XBC + D_SSD
C_C = C_B + SSD_GROUPS * SSD_STATE
C_DT = C_C + SSD_GROUPS * SSD_STATE
C_Z = C_DT + SSD_HEADS
N_IN = C_Z + D_SSD
N_IN_MAIN = (N_IN // LANES) * LANES
WPREP_ROWS = 256


def _wprep_kernel(w_ref, tail_ref, hg_ref, pool_ref, ssd_ref):
    rows = w_ref.shape[1]
    lane = lax.broadcasted_iota(jnp.int32, (rows, LANES), 1)
    tail = jnp.where(lane < N_IN - N_IN_MAIN, tail_ref[0], 0.0)
    hg_ref[0] = w_ref[0, :, 0:C_POOL].astype(BF16)
    pool_ref[0] = w_ref[0, :, C_POOL:C_XBC].astype(BF16)

    def window(a0, width):
        if a0 + width <= N_IN_MAIN:
            return w_ref[0, :, a0:a0 + width]
        assert a0 + width == N_IN_MAIN + LANES
        return jnp.concatenate([w_ref[0, :, a0:N_IN_MAIN], tail], axis=1)

    lane_g = lax.broadcasted_iota(jnp.int32, (rows, SSD_GWP), 1)

    def group(a):
        a0 = (a // LANES) * LANES
        win = window(a0, SSD_GWP + LANES)
        if a != a0:
            win = pltpu.roll(win, SSD_GWP + LANES - (a - a0), 1)
        return jnp.where(lane_g < SSD_GW, win[:, :SSD_GWP], 0.0)

    for g in range(SSD_GROUPS):
        ssd_ref[0, :, g * SSD_GWP:(g + 1) * SSD_GWP] = group(C_XBC + g * SSD_GW).astype(BF16)
        zo = SSD_XBC_P + g * SSD_GWP
        ssd_ref[0, :, zo:zo + SSD_GWP] = group(C_Z + g * SSD_GW).astype(BF16)
    ssd_ref[0, :, D_SSD_P:SSD_XBC_P] = w_ref[0, :, C_B:C_DT].astype(BF16)
    dt = jnp.where(lane < SSD_HEADS, window(C_DT, LANES), 0.0)
    ssd_ref[0, :, SSD_XBC_P + D_SSD_P:] = dt.astype(BF16)


def _wprep_call(w_in):
    depth, d, n_in = w_in.shape
    assert n_in == N_IN and C_DT % LANES == 0 and d % WPREP_ROWS == 0
    out = lambda width: pl.BlockSpec((1, WPREP_ROWS, width), lambda i, r: (i, r, 0))
    return pl.pallas_call(
        _wprep_kernel,
        grid=(depth, d // WPREP_ROWS),
        in_specs=[pl.BlockSpec((1, WPREP_ROWS, N_IN_MAIN), lambda i, r: (i, r, 0)),
                  pl.BlockSpec((1, WPREP_ROWS, LANES), lambda i, r: (i, r, N_IN_MAIN // LANES))],
        out_specs=(out(C_POOL), out(C_XBC - C_POOL), out(SSD_COLS)),
        out_shape=(jax.ShapeDtypeStruct((depth, d, C_POOL), BF16),
                   jax.ShapeDtypeStruct((depth, d, C_XBC - C_POOL), BF16),
                   jax.ShapeDtypeStruct((depth, d, SSD_COLS), BF16)),
        compiler_params=pltpu.CompilerParams(
            dimension_semantics=("parallel", "parallel"), vmem_limit_bytes=VMEM_LIMIT),
        name="weight_prep",
    )(w_in, w_in)


def kernel(x, p, norm_w, w_in, hg_lb, hg_norm_w, pool_w, pool_scale, conv_w, conv_b, dt_bias, a_log, d_skip,
           ssd_norm_w, w_out, w_pe, w_pg, final_norm_w):
    b, s, d = x.shape
    depth = w_in.shape[0]
    n = b * s
    t_hg = min(2048, s)
    t_pool = min(1024, s)
    t_ssd = min(512, s)
    tm = min(1024, n)

    lb_all = jnp.cumsum(jax.nn.softmax(hg_lb.astype(F32), axis=0), axis=0)
    lb_all = lb_all - lb_all[0]

    o_hg0, o_pool0 = 0, D_HG
    o_ssd0 = D_HG + D_POOL

    w_hg_all, w_pool_all, w_ssd_all = _wprep_call(w_in)

    h = x.reshape(n, d)
    u = _rms_norm_call(h, norm_w[0], tm).reshape(b, s, d)
    out = None
    for i in range(depth):
        lb = lb_all[i].reshape(1, D_HG)
        o_hg = _hgrn_call(u, w_hg_all[i], jnp.log(lb) * LOG2E, jnp.log1p(-lb) * LOG2E,
                          hg_norm_w[i].reshape(1, D_HG), t_hg)
        o_pool = _pool_call(u, w_pool_all[i], pool_w[i].astype(BF16),
                            pool_scale[i].reshape(1, D_POOL), t_pool)
        xbc_split = lambda a: jnp.concatenate([_pad_groups(a[..., :D_SSD]), a[..., D_SSD:]], axis=-1)
        rep = lambda a: _pad_groups(jnp.repeat(a, SSD_HEAD_DIM)).reshape(1, D_SSD_P)
        o_ssd = _ssd_call(u, w_ssd_all[i], xbc_split(conv_w[i]), xbc_split(conv_b[i]).reshape(1, SSD_XBC_P),
                          _pad_lanes(dt_bias[i]).reshape(1, LANES), _pad_lanes(a_log[i]).reshape(1, LANES),
                          rep(d_skip[i]), _pad_groups(ssd_norm_w[i]).reshape(1, D_SSD_P), t_ssd)
        wo = w_out[i]
        w3 = jnp.pad(wo[o_ssd0:].reshape(SSD_GROUPS, SSD_GW, d), ((0, 0), (0, SSD_GWP - SSD_GW), (0, 0)))
        last = i == depth - 1
        nw_next = final_norm_w if last else norm_w[i + 1]
        res = _out_call(h, o_hg.reshape(n, D_HG), o_pool.reshape(n, D_POOL), o_ssd.reshape(n, D_SSD_P),
                        p[i].reshape(n, P_DIM), wo[o_hg0:o_pool0].astype(BF16), wo[o_pool0:o_ssd0].astype(BF16),
                        w3.reshape(D_SSD_P, d).astype(BF16), w_pg[i].astype(BF16), w_pe[i].astype(BF16),
                        nw_next, tm, emit_h=not last)
        if last:
            out = res
        else:
            h, u = res
            u = u.reshape(b, s, d)
    return out.reshape(b, s, d)
```

```python
import functools
import math

import jax
import jax.numpy as jnp
from jax import lax
from jax.experimental import pallas as pl
from jax.experimental.pallas import tpu as pltpu

F32 = jnp.float32
BF16 = jnp.bfloat16

CHUNK = 64
EPS = 1e-6
D_MODEL = 1024
P_DIM = 256
HG_HEADS = 6
HG_D = 128
HG_PAIR = 2
HG_LEVELS = (64, 32, 16, 8, 4)
HG_HEAD_ROWS = 256
HG_GROUP = 16
HG_UNROLL = 32
LOG2E = 1.4426950408889634
D_HG = HG_HEADS * HG_D
POOL_WINDOWS = (2, 4, 8, 16)
POOL_CH = 128
D_POOL = len(POOL_WINDOWS) * POOL_CH
POOL_HALO = 16
SSD_HEADS = 12
SSD_HEAD_DIM = 64
SSD_GROUPS = 4
SSD_HPG = SSD_HEADS // SSD_GROUPS
SSD_GW = SSD_HPG * SSD_HEAD_DIM
SSD_GWP = 256
D_SSD = SSD_HEADS * SSD_HEAD_DIM
D_SSD_P = SSD_GROUPS * SSD_GWP
SSD_STATE = 128
SSD_CONV = 4
CONV_HALO = 8
SSD_SEL_ROWS = 256
SSD_UNROLL = 8
LANES = 128
SSD_XBC_P = D_SSD_P + 2 * SSD_GROUPS * SSD_STATE
SSD_COLS = SSD_XBC_P + D_SSD_P + LANES
VMEM_LIMIT = 48 * 1024 * 1024


def _dot(a, b):
    return jnp.dot(a.astype(BF16), b.astype(BF16), preferred_element_type=F32)


def _dot_nt(a, b):
    return lax.dot_general(a.astype(BF16), b.astype(BF16), (((1,), (1,)), ((), ())),
                           preferred_element_type=F32)


def _dot_tn(a, b):
    return lax.dot_general(a.astype(BF16), b.astype(BF16), (((0,), (0,)), ((), ())),
                           preferred_element_type=F32)


def _split3(x):
    hi = x.astype(BF16)
    r1 = x - hi.astype(F32)
    mid = r1.astype(BF16)
    lo = (r1 - mid.astype(F32)).astype(BF16)
    return hi, mid, lo


def _sel_left(m, x):
    hi, mid, lo = _split3(x)
    d = lambda p: jnp.dot(m, p, preferred_element_type=F32)
    return d(hi) + (d(mid) + d(lo))


def _sel_left3(m3, x):
    return jnp.dot(m3, jnp.concatenate(_split3(x), axis=0), preferred_element_type=F32)


def _sel_right3(x, m3):
    return jnp.dot(jnp.concatenate(_split3(x), axis=1), m3, preferred_element_type=F32)


def _chunk_rows(i):
    if isinstance(i, int):
        return pl.ds(i * CHUNK, CHUNK)
    return pl.ds(pl.multiple_of(i * CHUNK, CHUNK), CHUNK)


def _run_trips(n_trips, body):
    if n_trips == 1:
        body(0, 0)
    else:
        lax.fori_loop(0, n_trips, body, 0)


def _sigmoid(x):
    return 1.0 / (1.0 + jnp.exp(-x))


def _silu(x):
    return x * _sigmoid(x)


def _softplus(x):
    return jnp.maximum(x, 0.0) + jnp.log1p(jnp.exp(-jnp.abs(x)))


def _norm_kernel(x_ref, w_ref, o_ref):
    x = x_ref[...]
    y = x * lax.rsqrt(jnp.mean(x * x, axis=-1, keepdims=True) + EPS)
    o_ref[...] = (y * w_ref[...]).astype(o_ref.dtype)


def _rms_norm_call(x2d, w, tm):
    n, d = x2d.shape
    return pl.pallas_call(
        _norm_kernel,
        grid=(n // tm,),
        in_specs=[pl.BlockSpec((tm, d), lambda i: (i, 0)),
                  pl.BlockSpec((1, d), lambda i: (0, 0))],
        out_specs=pl.BlockSpec((tm, d), lambda i: (i, 0)),
        out_shape=jax.ShapeDtypeStruct((n, d), BF16),
        compiler_params=pltpu.CompilerParams(dimension_semantics=("parallel",)),
        name="rms_norm_in",
    )(x2d, w.reshape(1, d))


def _hgrn_kernel(u_ref, wq_ref, wf_ref, wv_ref, wg_ref, loglb_ref, log1mlb_ref, nw_ref, o_ref,
                 proj_ref, vg_ref, cum_ref, pair_ref, state_ref, *, n_chunks, unroll):
    @pl.when(pl.program_id(2) == 0)
    def _():
        state_ref[...] = jnp.zeros_like(state_ref)

    pw = HG_PAIR * HG_D
    w_refs = (wq_ref, wf_ref, wv_ref, wg_ref)

    def project(kind, r0, r1):
        dst, slot = (proj_ref, kind) if kind < 2 else (vg_ref, kind - 2)
        dst[r0:r1, slot * pw:(slot + 1) * pw] = jnp.dot(u_ref[0, r0:r1, :], w_refs[kind][...],
                                                        preferred_element_type=F32)

    t_blk = n_chunks * CHUNK
    for r0 in range(0, t_blk, HG_HEAD_ROWS):
        project(1, r0, min(r0 + HG_HEAD_ROWS, t_blk))
    for kind in (0, 2, 3):
        project(kind, 0, t_blk)

    row = lax.broadcasted_iota(jnp.int32, (CHUNK, HG_D), 0)
    sub = lax.broadcasted_iota(jnp.int32, (CHUNK // 8, 8, HG_D), 1)
    r64 = lax.broadcasted_iota(jnp.int32, (CHUNK, CHUNK), 0)
    c64 = lax.broadcasted_iota(jnp.int32, (CHUNK, CHUNK), 1)
    r3 = lax.broadcasted_iota(jnp.int32, (CHUNK, 3 * CHUNK), 0)
    c3 = lax.broadcasted_iota(jnp.int32, (CHUNK, 3 * CHUNK), 1)
    tril3 = (r3 >= (c3 % CHUNK)).astype(BF16)
    level_mask = {bs: ((r64 // bs) == (c64 // bs)) & ((r64 & (bs // 2)) != 0) & ((c64 & (bs // 2)) == 0)
                  for bs in HG_LEVELS}
    level_sign = {bs: jnp.where((row & (bs // 2)) != 0, 1.0, -1.0) for bs in HG_LEVELS}
    diag0 = r64 == c64
    diag1 = (r64 == c64 + 1) & ((r64 & 1) == 1)
    odd = (row & 1) == 1

    def ref_rows(cum, bs):
        half = bs // 2
        if bs >= 8:
            c3 = cum.reshape(CHUNK // bs, bs, HG_D)
            return jnp.broadcast_to(c3[:, half - 1:half, :], c3.shape).reshape(CHUNK, HG_D)
        c3 = cum.reshape(CHUNK // 8, 8, HG_D)
        lo = jnp.broadcast_to(c3[:, 1:2, :], c3.shape)
        hi = jnp.broadcast_to(c3[:, 5:6, :], c3.shape)
        return jnp.where(sub < 4, lo, hi).reshape(CHUNK, HG_D)

    def gates(c, carry):
        rows = _chunk_rows(c)
        for hh in range(HG_PAIR):
            lanes = slice(hh * HG_D, (hh + 1) * HG_D)
            z = proj_ref[rows, pw + hh * HG_D:pw + (hh + 1) * HG_D]
            loglb = loglb_ref[:, lanes]
            log1mlb = log1mlb_ref[:, lanes]
            zs = z * LOG2E
            log_sig = jnp.minimum(zs, 0.0) - jnp.log2(1.0 + jnp.exp2(-jnp.abs(zs)))
            b_ = log1mlb + log_sig
            log_f = jnp.maximum(loglb, b_) + jnp.log2(1.0 + jnp.exp2(-jnp.abs(loglb - b_)))
            f = jnp.exp2(log_f)
            kk = 1.0 - f
            proj_ref[rows, pw + hh * HG_D:pw + (hh + 1) * HG_D] = kk
            cum_ref[rows, lanes] = _sel_left3(tril3, log_f)
            pair_ref[rows, lanes] = jnp.where(odd, pltpu.roll(kk, 1, 0) * f, 0.0)
        return carry

    _run_trips(n_chunks // unroll, lambda c, carry: [gates(c * unroll + j, carry) for j in range(unroll)][-1])

    def trip(c, carry):
        for j0 in range(0, unroll, min(unroll, HG_GROUP)):
            stage_group(c, range(j0, j0 + min(unroll, HG_GROUP)))
        return carry

    def stage_group(c, chunk_ids):
        chains = [(j, hh) for j in chunk_ids for hh in range(HG_PAIR)]
        rows = {j: _chunk_rows(c * unroll + j) for j in chunk_ids}
        lanes = [slice(hh * HG_D, (hh + 1) * HG_D) for hh in range(HG_PAIR)]
        col = lambda kind, hh: slice(kind * pw + hh * HG_D, kind * pw + (hh + 1) * HG_D)
        q = {ch: proj_ref[rows[ch[0]], col(0, ch[1])] for ch in chains}
        kk = {ch: proj_ref[rows[ch[0]], col(1, ch[1])] for ch in chains}
        cum = {ch: cum_ref[rows[ch[0]], lanes[ch[1]]] for ch in chains}

        scores = {}
        for ch in chains:
            acc = None
            for bs in HG_LEVELS:
                e = jnp.exp2((cum[ch] - ref_rows(cum[ch], bs)) * level_sign[bs])
                m = jnp.where(level_mask[bs], _dot_nt(q[ch] * e, kk[ch] * e), 0.0)
                acc = m if acc is None else acc + m
            rs0 = jnp.sum(q[ch] * kk[ch], axis=1, keepdims=True)
            rs1 = jnp.sum(q[ch] * pair_ref[rows[ch[0]], lanes[ch[1]]], axis=1, keepdims=True)
            scores[ch] = acc + jnp.where(diag0, rs0, 0.0) + jnp.where(diag1, rs1, 0.0)

        inter = {}
        for ch in chains:
            j, hh = ch
            last = cum[ch][CHUNK - 1:CHUNK, :]
            st = state_ref[hh]
            inter[ch] = _dot_nt(q[ch] * jnp.exp2(cum[ch]), st)
            v = vg_ref[rows[j], col(0, hh)]
            state_ref[hh] = st * jnp.exp2(last) + _dot_tn(v, kk[ch] * jnp.exp2(last - cum[ch]))

        for ch in chains:
            j, hh = ch
            o = inter[ch] + _dot(scores[ch], vg_ref[rows[j], col(0, hh)])
            o = o * lax.rsqrt(jnp.mean(o * o, axis=-1, keepdims=True) + EPS) * nw_ref[:, lanes[hh]]
            g = vg_ref[rows[j], col(1, hh)]
            o_ref[0, rows[j], lanes[hh]] = (o * _silu(g)).astype(o_ref.dtype)

    _run_trips(n_chunks // unroll, trip)


def _hgrn_call(u, w_hg, layer, loglb, log1mlb, nw, t_blk):
    b, s, d = u.shape
    n_pairs = HG_HEADS // HG_PAIR
    pw = HG_PAIR * HG_D
    grid = (b, n_pairs, s // t_blk)
    vec = pl.BlockSpec((1, pw), lambda bi, hp, si: (0, hp))
    wspec = lambda kind: pl.BlockSpec((None, d, pw), lambda bi, hp, si: (layer, 0, kind * n_pairs + hp))
    return pl.pallas_call(
        functools.partial(_hgrn_kernel, n_chunks=t_blk // CHUNK, unroll=math.gcd(t_blk // CHUNK, HG_UNROLL)),
        grid=grid,
        in_specs=[pl.BlockSpec((1, t_blk, d), lambda bi, hp, si: (bi, si, 0)),
                  wspec(0), wspec(1), wspec(2), wspec(3),
                  vec, vec, vec],
        out_specs=pl.BlockSpec((1, t_blk, pw), lambda bi, hp, si: (bi, si, hp)),
        out_shape=jax.ShapeDtypeStruct((b, s, D_HG), BF16),
        scratch_shapes=[pltpu.VMEM((t_blk, 2 * pw), F32),
                        pltpu.VMEM((t_blk, 2 * pw), F32),
                        pltpu.VMEM((t_blk, pw), F32),
                        pltpu.VMEM((t_blk, pw), F32),
                        pltpu.VMEM((HG_PAIR, HG_D, HG_D), F32)],
        compiler_params=pltpu.CompilerParams(
            dimension_semantics=("parallel", "parallel", "arbitrary"),
            vmem_limit_bytes=VMEM_LIMIT),
        name="hgrn2_mixer",
    )(u, w_hg, w_hg, w_hg, w_hg, loglb, log1mlb, nw)


def _pool_kernel(u_ref, w_ref, pw_ref, ps_ref, o_ref, ext_ref, *, t_blk):
    si = pl.program_id(1)

    @pl.when(si == 0)
    def _():
        _pool_halo(ext_ref, t_blk, first=True)

    @pl.when(si > 0)
    def _():
        _pool_halo(ext_ref, t_blk, first=False)

    _pool_block(u_ref, w_ref, pw_ref, ps_ref, o_ref, ext_ref, si, t_blk)


def _pool_halo(ext_ref, t_blk, first):
    if first:
        ext_ref[0:POOL_HALO, :] = jnp.zeros((POOL_HALO, D_POOL), F32)
    else:
        ext_ref[0:POOL_HALO, :] = ext_ref[t_blk:t_blk + POOL_HALO, :]


def _pool_block(u_ref, w_ref, pw_ref, ps_ref, o_ref, ext_ref, si, t_blk):
    proj = jnp.dot(u_ref[0], w_ref[...], preferred_element_type=F32)
    ext_ref[POOL_HALO:POOL_HALO + t_blk, :] = proj[:, :D_POOL]

    pos = (si * t_blk + 1 + lax.broadcasted_iota(jnp.int32, (t_blk, POOL_CH), 0)).astype(F32)
    for gi, win in enumerate(POOL_WINDOWS):
        lo = gi * POOL_CH
        acc = ext_ref[:, lo:lo + POOL_CH]
        shift = 1
        while shift < win:
            acc = acc + pltpu.roll(acc, shift, 0)
            shift *= 2
        cur = proj[:, lo:lo + POOL_CH]
        pooled = acc[POOL_HALO:, :] / jnp.minimum(pos, float(win)) - cur
        y = _dot(pooled, pw_ref[gi]) * ps_ref[:, lo:lo + POOL_CH]
        gate = proj[:, D_POOL + lo:D_POOL + lo + POOL_CH]
        o_ref[0, :, lo:lo + POOL_CH] = (y * _silu(gate)).astype(o_ref.dtype)


def _pool_call(u, w_pool, layer, pool_w, pool_scale, t_blk):
    b, s, d = u.shape
    return pl.pallas_call(
        functools.partial(_pool_kernel, t_blk=t_blk),
        grid=(b, s // t_blk),
        in_specs=[pl.BlockSpec((1, t_blk, d), lambda bi, si: (bi, si, 0)),
                  pl.BlockSpec((None, d, 2 * D_POOL), lambda bi, si: (layer, 0, 0)),
                  pl.BlockSpec((len(POOL_WINDOWS), POOL_CH, POOL_CH), lambda bi, si: (0, 0, 0)),
                  pl.BlockSpec((1, D_POOL), lambda bi, si: (0, 0))],
        out_specs=pl.BlockSpec((1, t_blk, D_POOL), lambda bi, si: (bi, si, 0)),
        out_shape=jax.ShapeDtypeStruct((b, s, D_POOL), BF16),
        scratch_shapes=[pltpu.VMEM((t_blk + POOL_HALO, D_POOL), F32)],
        compiler_params=pltpu.CompilerParams(
            dimension_semantics=("parallel", "arbitrary"),
            vmem_limit_bytes=VMEM_LIMIT),
        name="pool_mixer",
    )(u, w_pool, pool_w, pool_scale)


def _ssd_kernel(u_ref, w_ref, cw_ref, cb_ref, dtb_ref, alogc_ref, dskip_ref, nw_ref,
                exp_ref, btril_ref, dup_ref, o_ref,
                ext_ref, act_ref, z_ref, xdt_ref, cume_ref, cumt_ref, state_ref, *, t_blk, unroll):
    si = pl.program_id(1)
    n_chunks = t_blk // CHUNK

    @pl.when(si == 0)
    def _():
        ext_ref[0:CONV_HALO, :] = jnp.zeros((CONV_HALO, SSD_XBC_P), F32)
        state_ref[...] = jnp.zeros_like(state_ref)

    @pl.when(si > 0)
    def _():
        ext_ref[0:CONV_HALO, :] = ext_ref[t_blk:t_blk + CONV_HALO, :]

    proj = jnp.dot(u_ref[0], w_ref[...], preferred_element_type=F32)
    ext_ref[CONV_HALO:CONV_HALO + t_blk, :] = proj[:, :SSD_XBC_P]
    z_ref[...] = proj[:, SSD_XBC_P:SSD_XBC_P + D_SSD_P]

    assert SSD_CONV == 4
    e = ext_ref[...]
    e2 = pltpu.roll(e, 2, 0)
    even = cw_ref[3:4, :] * e + cw_ref[1:2, :] * e2
    odd = cw_ref[2:3, :] * e + cw_ref[0:1, :] * e2
    xc = (even + pltpu.roll(odd, 1, 0))[CONV_HALO:, :] + cb_ref[...]
    xc = _silu(xc)
    act_ref[...] = xc

    dt = _softplus(proj[:, SSD_XBC_P + D_SSD_P:] + dtb_ref[...])
    exp3 = jnp.concatenate([exp_ref[...]] * 3, axis=0)
    xdt_ref[...] = xc[:, :D_SSD_P] * _sel_right3(dt, exp3)

    dta = dt * (-jnp.exp(alogc_ref[...]) * LOG2E)
    sub = btril_ref.shape[0]
    dup3 = jnp.concatenate([dup_ref[...]] * 3, axis=0)
    for r0 in range(0, t_blk, sub):
        cum_c = _sel_left(btril_ref[...], dta[r0:r0 + sub, :])
        hi, mid, lo = _split3(cum_c)
        cume_ref[r0:r0 + sub, :] = jnp.dot(jnp.concatenate([hi, mid, lo], axis=1), exp3,
                                           preferred_element_type=F32)
        cum_t = lax.dot_general(jnp.concatenate([hi, mid, lo], axis=0), dup3,
                                (((0,), (0,)), ((), ())), preferred_element_type=F32)
        for c in range(sub // CHUNK):
            cumt_ref[r0 // CHUNK + c] = cum_t[:, c * LANES:(c + 1) * LANES]

    r64 = lax.broadcasted_iota(jnp.int32, (CHUNK, CHUNK), 0)
    c64 = lax.broadcasted_iota(jnp.int32, (CHUNK, CHUNK), 1)
    causal = r64 >= c64
    r128 = lax.broadcasted_iota(jnp.int32, (CHUNK, LANES), 0)
    l128 = lax.broadcasted_iota(jnp.int32, (CHUNK, LANES), 1)
    causal2 = r128 >= (l128 & (CHUNK - 1))
    first_half = lax.broadcasted_iota(jnp.int32, (1, LANES), 1) < CHUNK
    rb = lax.broadcasted_iota(jnp.int32, (LANES, LANES), 0)
    lb = lax.broadcasted_iota(jnp.int32, (LANES, LANES), 1)
    blockdiag = (rb >= CHUNK) == (lb >= CHUNK)
    b_off = D_SSD_P
    c_off = D_SSD_P + SSD_GROUPS * SSD_STATE
    groups = range(SSD_GROUPS)

    gl = [slice(g * SSD_GWP, (g + 1) * SSD_GWP) for g in groups]
    bl = [slice(b_off + g * SSD_STATE, b_off + (g + 1) * SSD_STATE) for g in groups]
    cl = [slice(c_off + g * SSD_STATE, c_off + (g + 1) * SSD_STATE) for g in groups]

    def trip(c, carry):
        chains = [(j, g) for j in range(unroll) for g in groups]
        rows = [_chunk_rows(c * unroll + j) for j in range(unroll)]
        cum_t = [cumt_ref[c * unroll + j] for j in range(unroll)]

        cb2 = {}
        for j, g in chains:
            bm = act_ref[rows[j], bl[g]]
            cb2[j, g] = _dot_nt(act_ref[rows[j], cl[g]], jnp.concatenate([bm, bm], axis=0))
        y_diag = {}
        for j, g in chains:
            hd = g * SSD_HPG
            cs01 = jnp.where(first_half, cum_t[j][hd:hd + 1, :], cum_t[j][hd + 1:hd + 2, :])
            ct01 = cume_ref[rows[j], g * SSD_GWP:g * SSD_GWP + LANES]
            seg01 = jnp.exp2(jnp.where(causal2, ct01 - cs01, -jnp.inf))
            ct2 = cume_ref[rows[j], g * SSD_GWP + LANES:g * SSD_GWP + LANES + CHUNK]
            seg2 = jnp.exp2(jnp.where(causal, ct2 - cum_t[j][hd + 2:hd + 3, 0:CHUNK], -jnp.inf))
            x01 = xdt_ref[rows[j], g * SSD_GWP:g * SSD_GWP + LANES]
            xbd = jnp.where(blockdiag, jnp.concatenate([x01, x01], axis=0), 0.0)
            y01 = _dot(cb2[j, g] * seg01, xbd)
            y2 = _dot(cb2[j, g][:, :CHUNK] * seg2, xdt_ref[rows[j], g * SSD_GWP + LANES:(g + 1) * SSD_GWP])
            y_diag[j, g] = jnp.concatenate([y01, y2], axis=1)

        y_off = {}
        for j, g in chains:
            cum_e = cume_ref[rows[j], gl[g]]
            last_e = cum_e[CHUNK - 1:CHUNK, :]
            hst = state_ref[g]
            y_off[j, g] = _dot(act_ref[rows[j], cl[g]], hst) * jnp.exp2(cum_e)
            xdec = xdt_ref[rows[j], gl[g]] * jnp.exp2(last_e - cum_e)
            state_ref[g] = hst * jnp.exp2(last_e) + _dot_tn(act_ref[rows[j], bl[g]], xdec)

        for j, g in chains:
            y = y_diag[j, g] + y_off[j, g] + act_ref[rows[j], gl[g]] * dskip_ref[:, gl[g]]
            y = y * _silu(z_ref[rows[j], gl[g]])
            ms = jnp.sum(y * y, axis=-1, keepdims=True) * (1.0 / SSD_GW)
            y = y * lax.rsqrt(ms + EPS) * nw_ref[:, gl[g]]
            o_ref[0, rows[j], gl[g]] = y.astype(o_ref.dtype)
        return carry

    _run_trips(n_chunks // unroll, trip)


def _ssd_call(u, w_ssd, layer, conv_w, conv_b, dt_bias, alog_c, dskip_e, nw_e, t_blk):
    b, s, d = u.shape
    n_chunks = t_blk // CHUNK
    heads = jnp.arange(SSD_HEADS)
    lanes_e = (heads // SSD_HPG) * SSD_GWP + (heads % SSD_HPG) * SSD_HEAD_DIM
    col_e = jnp.arange(D_SSD_P)[None, :]
    expand = ((col_e >= lanes_e[:, None]) & (col_e < lanes_e[:, None] + SSD_HEAD_DIM))
    expand = jnp.pad(expand, ((0, LANES - SSD_HEADS), (0, 0))).astype(BF16)
    sub = math.gcd(t_blk, SSD_SEL_ROWS)
    ti = jnp.arange(sub)
    btril = ((ti[:, None] >= ti[None, :]) & (ti[:, None] // CHUNK == ti[None, :] // CHUNK)).astype(BF16)
    cj = jnp.arange(2 * sub)
    dup = (ti[:, None] == (cj[None, :] // LANES) * CHUNK + cj[None, :] % CHUNK).astype(BF16)

    full = lambda shape: pl.BlockSpec(shape, lambda bi, si: (0,) * len(shape))
    return pl.pallas_call(
        functools.partial(_ssd_kernel, t_blk=t_blk, unroll=math.gcd(n_chunks, SSD_UNROLL)),
        grid=(b, s // t_blk),
        in_specs=[pl.BlockSpec((1, t_blk, d), lambda bi, si: (bi, si, 0)),
                  pl.BlockSpec((None, d, SSD_COLS), lambda bi, si: (layer, 0, 0)),
                  full((SSD_CONV, SSD_XBC_P)), full((1, SSD_XBC_P)),
                  full((1, LANES)), full((1, LANES)),
                  full((1, D_SSD_P)), full((1, D_SSD_P)),
                  full((LANES, D_SSD_P)), full((sub, sub)), full((sub, 2 * sub))],
        out_specs=pl.BlockSpec((1, t_blk, D_SSD_P), lambda bi, si: (bi, si, 0)),
        out_shape=jax.ShapeDtypeStruct((b, s, D_SSD_P), BF16),
        scratch_shapes=[pltpu.VMEM((t_blk + CONV_HALO, SSD_XBC_P), F32),
                        pltpu.VMEM((t_blk, SSD_XBC_P), F32),
                        pltpu.VMEM((t_blk, D_SSD_P), F32),
                        pltpu.VMEM((t_blk, D_SSD_P), F32),
                        pltpu.VMEM((t_blk, D_SSD_P), F32),
                        pltpu.VMEM((n_chunks, LANES, LANES), F32),
                        pltpu.VMEM((SSD_GROUPS, SSD_STATE, SSD_GWP), F32)],
        compiler_params=pltpu.CompilerParams(
            dimension_semantics=("parallel", "arbitrary"),
            vmem_limit_bytes=VMEM_LIMIT),
        name="ssd_mixer",
    )(u, w_ssd, conv_w, conv_b, dt_bias, alog_c, dskip_e, nw_e, expand, btril, dup)


def _out_kernel(h_ref, ohg_ref, opool_ref, ossd_ref, p_ref, w1_ref, w2_ref, w3_ref, wpg_ref, wpe_ref,
                nw_ref, *out_refs, emit_h):
    dot = lambda a, w: jnp.dot(a, w, preferred_element_type=F32)
    acc = h_ref[...] + dot(ohg_ref[...], w1_ref[...]) + dot(opool_ref[...], w2_ref[...]) \
        + dot(ossd_ref[...], w3_ref[...])
    gate = _sigmoid(_dot(acc, wpg_ref[...]))
    hn = acc + gate * _dot(p_ref[...], wpe_ref[...])
    y = hn * lax.rsqrt(jnp.mean(hn * hn, axis=-1, keepdims=True) + EPS) * nw_ref[...]
    if emit_h:
        out_refs[0][...] = hn
        out_refs[1][...] = y.astype(out_refs[1].dtype)
    else:
        out_refs[0][...] = y


def _out_call(h2d, ohg, opool, ossd, p2d, w1, w2, w3, wpg, wpe, layer, nw, tm, emit_h):
    n, d = h2d.shape
    rows = lambda width: pl.BlockSpec((tm, width), lambda i: (i, 0))
    full = lambda a: pl.BlockSpec((None,) + a.shape[1:], lambda i: (layer, 0, 0),
                                  pipeline_mode=pl.Buffered(1))
    if emit_h:
        out_shape = (jax.ShapeDtypeStruct((n, d), F32), jax.ShapeDtypeStruct((n, d), BF16))
        out_specs = (rows(d), rows(d))
    else:
        out_shape = jax.ShapeDtypeStruct((n, d), F32)
        out_specs = rows(d)
    return pl.pallas_call(
        functools.partial(_out_kernel, emit_h=emit_h),
        grid=(n // tm,),
        in_specs=[rows(d), rows(D_HG), rows(D_POOL), rows(D_SSD_P), rows(P_DIM),
                  full(w1), full(w2), full(w3), full(wpg), full(wpe), pl.BlockSpec((1, d), lambda i: (0, 0))],
        out_specs=out_specs,
        out_shape=out_shape,
        compiler_params=pltpu.CompilerParams(
            dimension_semantics=("parallel",), vmem_limit_bytes=VMEM_LIMIT),
        name="out_proj",
    )(h2d, ohg, opool, ossd, p2d, w1, w2, w3, wpg, wpe, nw.reshape(1, d))


def _pad_groups(a):
    lead = a.shape[:-1]
    a = a.reshape(lead + (SSD_GROUPS, SSD_GW))
    a = jnp.pad(a, [(0, 0)] * len(lead) + [(0, 0), (0, SSD_GWP - SSD_GW)])
    return a.reshape(lead + (D_SSD_P,))


def _pad_lanes(a):
    return jnp.pad(a, [(0, 0)] * (a.ndim - 1) + [(0, LANES - a.shape[-1])])


C_POOL = 4 * D_HG
C_XBC = C_POOL + 2 * D_POOL
C_B = C_XBC + D_SSD
C_C = C_B + SSD_GROUPS * SSD_STATE
C_DT = C_C + SSD_GROUPS * SSD_STATE
C_Z = C_DT + SSD_HEADS
N_IN = C_Z + D_SSD
N_IN_MAIN = (N_IN // LANES) * LANES
WPREP_ROWS = 256


def _wprep_kernel(w_ref, tail_ref, hg_ref, pool_ref, ssd_ref):
    rows = w_ref.shape[1]
    lane = lax.broadcasted_iota(jnp.int32, (rows, LANES), 1)
    tail = jnp.where(lane < N_IN - N_IN_MAIN, tail_ref[0], 0.0)
    hg_ref[0] = w_ref[0, :, 0:C_POOL].astype(BF16)
    pool_ref[0] = w_ref[0, :, C_POOL:C_XBC].astype(BF16)

    def window(a0, width):
        if a0 + width <= N_IN_MAIN:
            return w_ref[0, :, a0:a0 + width]
        assert a0 + width == N_IN_MAIN + LANES
        return jnp.concatenate([w_ref[0, :, a0:N_IN_MAIN], tail], axis=1)

    lane_g = lax.broadcasted_iota(jnp.int32, (rows, SSD_GWP), 1)

    def group(a):
        a0 = (a // LANES) * LANES
        win = window(a0, SSD_GWP + LANES)
        if a != a0:
            win = pltpu.roll(win, SSD_GWP + LANES - (a - a0), 1)
        return jnp.where(lane_g < SSD_GW, win[:, :SSD_GWP], 0.0)

    for g in range(SSD_GROUPS):
        ssd_ref[0, :, g * SSD_GWP:(g + 1) * SSD_GWP] = group(C_XBC + g * SSD_GW).astype(BF16)
        zo = SSD_XBC_P + g * SSD_GWP
        ssd_ref[0, :, zo:zo + SSD_GWP] = group(C_Z + g * SSD_GW).astype(BF16)
    ssd_ref[0, :, D_SSD_P:SSD_XBC_P] = w_ref[0, :, C_B:C_DT].astype(BF16)
    dt = jnp.where(lane < SSD_HEADS, window(C_DT, LANES), 0.0)
    ssd_ref[0, :, SSD_XBC_P + D_SSD_P:] = dt.astype(BF16)


def _wprep_call(w_in):
    depth, d, n_in = w_in.shape
    assert n_in == N_IN and C_DT % LANES == 0 and d % WPREP_ROWS == 0
    out = lambda width: pl.BlockSpec((1, WPREP_ROWS, width), lambda i, r: (i, r, 0))
    return pl.pallas_call(
        _wprep_kernel,
        grid=(depth, d // WPREP_ROWS),
        in_specs=[pl.BlockSpec((1, WPREP_ROWS, N_IN_MAIN), lambda i, r: (i, r, 0)),
                  pl.BlockSpec((1, WPREP_ROWS, LANES), lambda i, r: (i, r, N_IN_MAIN // LANES))],
        out_specs=(out(C_POOL), out(C_XBC - C_POOL), out(SSD_COLS)),
        out_shape=(jax.ShapeDtypeStruct((depth, d, C_POOL), BF16),
                   jax.ShapeDtypeStruct((depth, d, C_XBC - C_POOL), BF16),
                   jax.ShapeDtypeStruct((depth, d, SSD_COLS), BF16)),
        compiler_params=pltpu.CompilerParams(
            dimension_semantics=("parallel", "parallel"), vmem_limit_bytes=VMEM_LIMIT),
        name="weight_prep",
    )(w_in, w_in)


def _wcast_kernel(wo_ref, wpg_ref, wpe_ref, o1_ref, o2_ref, o3_ref, opg_ref, ope_ref):
    o1_ref[0] = wo_ref[0, 0:D_HG, :].astype(BF16)
    o2_ref[0] = wo_ref[0, D_HG:D_HG + D_POOL, :].astype(BF16)
    base = D_HG + D_POOL
    for g in range(SSD_GROUPS):
        o3_ref[0, g * SSD_GWP:g * SSD_GWP + SSD_GW, :] = \
            wo_ref[0, base + g * SSD_GW:base + (g + 1) * SSD_GW, :].astype(BF16)
        o3_ref[0, g * SSD_GWP + SSD_GW:(g + 1) * SSD_GWP, :] = jnp.zeros((SSD_GWP - SSD_GW, wo_ref.shape[2]), BF16)
    opg_ref[0] = wpg_ref[0].astype(BF16)
    ope_ref[0] = wpe_ref[0].astype(BF16)


def _wcast_call(w_out, w_pg, w_pe):
    depth, d_mix, d = w_out.shape
    assert d_mix == D_HG + D_POOL + D_SSD
    layer = lambda a: pl.BlockSpec((1,) + a.shape[1:], lambda i: (i, 0, 0))
    shapes = [(depth, D_HG, d), (depth, D_POOL, d), (depth, D_SSD_P, d), w_pg.shape, w_pe.shape]
    outs = [jax.ShapeDtypeStruct(sh, BF16) for sh in shapes]
    return pl.pallas_call(
        _wcast_kernel,
        grid=(depth,),
        in_specs=[layer(w_out), layer(w_pg), layer(w_pe)],
        out_specs=tuple(layer(o) for o in outs),
        out_shape=tuple(outs),
        compiler_params=pltpu.CompilerParams(dimension_semantics=("parallel",), vmem_limit_bytes=VMEM_LIMIT),
        name="weight_cast",
    )(w_out, w_pg, w_pe)


def kernel(x, p, norm_w, w_in, hg_lb, hg_norm_w, pool_w, pool_scale, conv_w, conv_b, dt_bias, a_log, d_skip,
           ssd_norm_w, w_out, w_pe, w_pg, final_norm_w):
    b, s, d = x.shape
    depth = w_in.shape[0]
    n = b * s
    t_hg = min(2048, s)
    t_pool = min(1024, s)
    t_ssd = min(512, s)
    tm = min(1024, n)

    lb_all = jnp.cumsum(jax.nn.softmax(hg_lb.astype(F32), axis=0), axis=0)
    lb_all = lb_all - lb_all[0]


    w_hg_all, w_pool_all, w_ssd_all = _wprep_call(w_in)
    w_out_parts = _wcast_call(w_out, w_pg, w_pe)

    h = x.reshape(n, d)
    u = _rms_norm_call(h, norm_w[0], tm).reshape(b, s, d)
    out = None
    for i in range(depth):
        lb = lb_all[i].reshape(1, D_HG)
        o_hg = _hgrn_call(u, w_hg_all, i, jnp.log(lb) * LOG2E, jnp.log1p(-lb) * LOG2E,
                          hg_norm_w[i].reshape(1, D_HG), t_hg)
        o_pool = _pool_call(u, w_pool_all, i, pool_w[i].astype(BF16),
                            pool_scale[i].reshape(1, D_POOL), t_pool)
        xbc_split = lambda a: jnp.concatenate([_pad_groups(a[..., :D_SSD]), a[..., D_SSD:]], axis=-1)
        rep = lambda a: _pad_groups(jnp.repeat(a, SSD_HEAD_DIM)).reshape(1, D_SSD_P)
        o_ssd = _ssd_call(u, w_ssd_all, i, xbc_split(conv_w[i]), xbc_split(conv_b[i]).reshape(1, SSD_XBC_P),
                          _pad_lanes(dt_bias[i]).reshape(1, LANES), _pad_lanes(a_log[i]).reshape(1, LANES),
                          rep(d_skip[i]), _pad_groups(ssd_norm_w[i]).reshape(1, D_SSD_P), t_ssd)
        last = i == depth - 1
        nw_next = final_norm_w if last else norm_w[i + 1]
        res = _out_call(h, o_hg.reshape(n, D_HG), o_pool.reshape(n, D_POOL), o_ssd.reshape(n, D_SSD_P),
                        p[i].reshape(n, P_DIM), *w_out_parts, i, nw_next, tm, emit_h=not last)
        if last:
            out = res
        else:
            h, u = res
            u = u.reshape(b, s, d)
    return out.reshape(b, s, d)
```

```python
import functools
import math

import jax
import jax.numpy as jnp
from jax import lax
from jax.experimental import pallas as pl
from jax.experimental.pallas import tpu as pltpu

F32 = jnp.float32
BF16 = jnp.bfloat16

CHUNK = 64
EPS = 1e-6
D_MODEL = 1024
P_DIM = 256
HG_HEADS = 6
HG_D = 128
HG_PAIR = 2
HG_LEVELS = (64, 32, 16, 8, 4)
HG_HEAD_ROWS = 256
HG_GROUP = 16
HG_UNROLL = 32
LOG2E = 1.4426950408889634
D_HG = HG_HEADS * HG_D
POOL_WINDOWS = (2, 4, 8, 16)
POOL_CH = 128
D_POOL = len(POOL_WINDOWS) * POOL_CH
POOL_HALO = 16
SSD_HEADS = 12
SSD_HEAD_DIM = 64
SSD_GROUPS = 4
SSD_HPG = SSD_HEADS // SSD_GROUPS
SSD_GW = SSD_HPG * SSD_HEAD_DIM
SSD_GWP = 256
D_SSD = SSD_HEADS * SSD_HEAD_DIM
D_SSD_P = SSD_GROUPS * SSD_GWP
SSD_STATE = 128
SSD_CONV = 4
CONV_HALO = 8
SSD_SEL_ROWS = 256
SSD_UNROLL = 8
LANES = 128
SSD_XBC_P = D_SSD_P + 2 * SSD_GROUPS * SSD_STATE
SSD_COLS = SSD_XBC_P + D_SSD_P + LANES
VMEM_LIMIT = 48 * 1024 * 1024


def _dot(a, b):
    return jnp.dot(a.astype(BF16), b.astype(BF16), preferred_element_type=F32)


def _dot_nt(a, b):
    return lax.dot_general(a.astype(BF16), b.astype(BF16), (((1,), (1,)), ((), ())),
                           preferred_element_type=F32)


def _dot_tn(a, b):
    return lax.dot_general(a.astype(BF16), b.astype(BF16), (((0,), (0,)), ((), ())),
                           preferred_element_type=F32)


def _split3(x):
    hi = x.astype(BF16)
    r1 = x - hi.astype(F32)
    mid = r1.astype(BF16)
    lo = (r1 - mid.astype(F32)).astype(BF16)
    return hi, mid, lo


def _sel_left(m, x):
    hi, mid, lo = _split3(x)
    d = lambda p: jnp.dot(m, p, preferred_element_type=F32)
    return d(hi) + (d(mid) + d(lo))


def _sel_left3(m3, x):
    return jnp.dot(m3, jnp.concatenate(_split3(x), axis=0), preferred_element_type=F32)


def _sel_right3(x, m3):
    return jnp.dot(jnp.concatenate(_split3(x), axis=1), m3, preferred_element_type=F32)


def _chunk_rows(i):
    if isinstance(i, int):
        return pl.ds(i * CHUNK, CHUNK)
    return pl.ds(pl.multiple_of(i * CHUNK, CHUNK), CHUNK)


def _run_trips(n_trips, body):
    if n_trips == 1:
        body(0, 0)
    else:
        lax.fori_loop(0, n_trips, body, 0)


def _sigmoid(x):
    return 1.0 / (1.0 + jnp.exp(-x))


def _silu(x):
    return x * _sigmoid(x)


def _softplus(x):
    return jnp.maximum(x, 0.0) + jnp.log1p(jnp.exp(-jnp.abs(x)))


def _norm_kernel(x_ref, w_ref, o_ref):
    x = x_ref[...]
    y = x * lax.rsqrt(jnp.mean(x * x, axis=-1, keepdims=True) + EPS)
    o_ref[...] = (y * w_ref[...]).astype(o_ref.dtype)


def _rms_norm_call(x2d, w, tm):
    n, d = x2d.shape
    return pl.pallas_call(
        _norm_kernel,
        grid=(n // tm,),
        in_specs=[pl.BlockSpec((tm, d), lambda i: (i, 0)),
                  pl.BlockSpec((1, d), lambda i: (0, 0))],
        out_specs=pl.BlockSpec((tm, d), lambda i: (i, 0)),
        out_shape=jax.ShapeDtypeStruct((n, d), BF16),
        compiler_params=pltpu.CompilerParams(dimension_semantics=("parallel",)),
        name="rms_norm_in",
    )(x2d, w.reshape(1, d))


def _hgrn_kernel(u_ref, wq_ref, wf_ref, wv_ref, wg_ref, loglb_ref, log1mlb_ref, nw_ref, o_ref,
                 proj_ref, vg_ref, cum_ref, pair_ref, state_ref, *, n_chunks, unroll):
    @pl.when(pl.program_id(2) == 0)
    def _():
        state_ref[...] = jnp.zeros_like(state_ref)

    pw = HG_PAIR * HG_D
    w_refs = (wq_ref, wf_ref, wv_ref, wg_ref)

    def project(kind, r0, r1):
        dst, slot = (proj_ref, kind) if kind < 2 else (vg_ref, kind - 2)
        dst[r0:r1, slot * pw:(slot + 1) * pw] = jnp.dot(u_ref[0, r0:r1, :], w_refs[kind][...],
                                                        preferred_element_type=F32)

    t_blk = n_chunks * CHUNK
    for r0 in range(0, t_blk, HG_HEAD_ROWS):
        project(1, r0, min(r0 + HG_HEAD_ROWS, t_blk))
    for kind in (0, 2, 3):
        project(kind, 0, t_blk)

    row = lax.broadcasted_iota(jnp.int32, (CHUNK, HG_D), 0)
    sub = lax.broadcasted_iota(jnp.int32, (CHUNK // 8, 8, HG_D), 1)
    r64 = lax.broadcasted_iota(jnp.int32, (CHUNK, CHUNK), 0)
    c64 = lax.broadcasted_iota(jnp.int32, (CHUNK, CHUNK), 1)
    r3 = lax.broadcasted_iota(jnp.int32, (CHUNK, 3 * CHUNK), 0)
    c3 = lax.broadcasted_iota(jnp.int32, (CHUNK, 3 * CHUNK), 1)
    tril3 = (r3 >= (c3 % CHUNK)).astype(BF16)
    level_mask = {bs: ((r64 // bs) == (c64 // bs)) & ((r64 & (bs // 2)) != 0) & ((c64 & (bs // 2)) == 0)
                  for bs in HG_LEVELS}
    level_sign = {bs: jnp.where((row & (bs // 2)) != 0, 1.0, -1.0) for bs in HG_LEVELS}
    diag0 = r64 == c64
    diag1 = (r64 == c64 + 1) & ((r64 & 1) == 1)
    odd = (row & 1) == 1

    def ref_rows(cum, bs):
        half = bs // 2
        if bs >= 8:
            c3 = cum.reshape(CHUNK // bs, bs, HG_D)
            return jnp.broadcast_to(c3[:, half - 1:half, :], c3.shape).reshape(CHUNK, HG_D)
        c3 = cum.reshape(CHUNK // 8, 8, HG_D)
        lo = jnp.broadcast_to(c3[:, 1:2, :], c3.shape)
        hi = jnp.broadcast_to(c3[:, 5:6, :], c3.shape)
        return jnp.where(sub < 4, lo, hi).reshape(CHUNK, HG_D)

    def gates(c, carry):
        rows = _chunk_rows(c)
        for hh in range(HG_PAIR):
            lanes = slice(hh * HG_D, (hh + 1) * HG_D)
            z = proj_ref[rows, pw + hh * HG_D:pw + (hh + 1) * HG_D]
            loglb = loglb_ref[:, lanes]
            log1mlb = log1mlb_ref[:, lanes]
            zs = z * LOG2E
            log_sig = jnp.minimum(zs, 0.0) - jnp.log2(1.0 + jnp.exp2(-jnp.abs(zs)))
            b_ = log1mlb + log_sig
            log_f = jnp.maximum(loglb, b_) + jnp.log2(1.0 + jnp.exp2(-jnp.abs(loglb - b_)))
            f = jnp.exp2(log_f)
            kk = 1.0 - f
            proj_ref[rows, pw + hh * HG_D:pw + (hh + 1) * HG_D] = kk
            cum_ref[rows, lanes] = _sel_left3(tril3, log_f)
            pair_ref[rows, lanes] = jnp.where(odd, pltpu.roll(kk, 1, 0) * f, 0.0)
        return carry

    _run_trips(n_chunks // unroll, lambda c, carry: [gates(c * unroll + j, carry) for j in range(unroll)][-1])

    def trip(c, carry):
        for j0 in range(0, unroll, min(unroll, HG_GROUP)):
            stage_group(c, range(j0, j0 + min(unroll, HG_GROUP)))
        return carry

    def stage_group(c, chunk_ids):
        chains = [(j, hh) for j in chunk_ids for hh in range(HG_PAIR)]
        rows = {j: _chunk_rows(c * unroll + j) for j in chunk_ids}
        lanes = [slice(hh * HG_D, (hh + 1) * HG_D) for hh in range(HG_PAIR)]
        col = lambda kind, hh: slice(kind * pw + hh * HG_D, kind * pw + (hh + 1) * HG_D)
        q = {ch: proj_ref[rows[ch[0]], col(0, ch[1])] for ch in chains}
        kk = {ch: proj_ref[rows[ch[0]], col(1, ch[1])] for ch in chains}
        cum = {ch: cum_ref[rows[ch[0]], lanes[ch[1]]] for ch in chains}

        scores = {}
        for ch in chains:
            acc = None
            for bs in HG_LEVELS:
                e = jnp.exp2((cum[ch] - ref_rows(cum[ch], bs)) * level_sign[bs])
                m = jnp.where(level_mask[bs], _dot_nt(q[ch] * e, kk[ch] * e), 0.0)
                acc = m if acc is None else acc + m
            rs0 = jnp.sum(q[ch] * kk[ch], axis=1, keepdims=True)
            rs1 = jnp.sum(q[ch] * pair_ref[rows[ch[0]], lanes[ch[1]]], axis=1, keepdims=True)
            scores[ch] = acc + jnp.where(diag0, rs0, 0.0) + jnp.where(diag1, rs1, 0.0)

        inter = {}
        for ch in chains:
            j, hh = ch
            last = cum[ch][CHUNK - 1:CHUNK, :]
            st = state_ref[hh]
            inter[ch] = _dot_nt(q[ch] * jnp.exp2(cum[ch]), st)
            v = vg_ref[rows[j], col(0, hh)]
            state_ref[hh] = st * jnp.exp2(last) + _dot_tn(v, kk[ch] * jnp.exp2(last - cum[ch]))

        for ch in chains:
            j, hh = ch
            o = inter[ch] + _dot(scores[ch], vg_ref[rows[j], col(0, hh)])
            o = o * lax.rsqrt(jnp.mean(o * o, axis=-1, keepdims=True) + EPS) * nw_ref[:, lanes[hh]]
            g = vg_ref[rows[j], col(1, hh)]
            o_ref[0, rows[j], lanes[hh]] = (o * _silu(g)).astype(o_ref.dtype)

    _run_trips(n_chunks // unroll, trip)


def _hgrn_call(u, w_hg, layer, loglb, log1mlb, nw, t_blk):
    b, s, d = u.shape
    n_pairs = HG_HEADS // HG_PAIR
    pw = HG_PAIR * HG_D
    grid = (b, n_pairs, s // t_blk)
    vec = pl.BlockSpec((1, pw), lambda bi, hp, si: (0, hp))
    wspec = lambda kind: pl.BlockSpec((None, d, pw), lambda bi, hp, si: (layer, 0, kind * n_pairs + hp))
    return pl.pallas_call(
        functools.partial(_hgrn_kernel, n_chunks=t_blk // CHUNK, unroll=math.gcd(t_blk // CHUNK, HG_UNROLL)),
        grid=grid,
        in_specs=[pl.BlockSpec((1, t_blk, d), lambda bi, hp, si: (bi, si, 0)),
                  wspec(0), wspec(1), wspec(2), wspec(3),
                  vec, vec, vec],
        out_specs=pl.BlockSpec((1, t_blk, pw), lambda bi, hp, si: (bi, si, hp)),
        out_shape=jax.ShapeDtypeStruct((b, s, D_HG), BF16),
        scratch_shapes=[pltpu.VMEM((t_blk, 2 * pw), F32),
                        pltpu.VMEM((t_blk, 2 * pw), F32),
                        pltpu.VMEM((t_blk, pw), F32),
                        pltpu.VMEM((t_blk, pw), F32),
                        pltpu.VMEM((HG_PAIR, HG_D, HG_D), F32)],
        compiler_params=pltpu.CompilerParams(
            dimension_semantics=("parallel", "parallel", "arbitrary"),
            vmem_limit_bytes=VMEM_LIMIT),
        name="hgrn2_mixer",
    )(u, w_hg, w_hg, w_hg, w_hg, loglb, log1mlb, nw)


def _pool_kernel(u_ref, w_ref, pw_ref, ps_ref, o_ref, ext_ref, *, t_blk):
    si = pl.program_id(1)

    @pl.when(si == 0)
    def _():
        _pool_halo(ext_ref, t_blk, first=True)

    @pl.when(si > 0)
    def _():
        _pool_halo(ext_ref, t_blk, first=False)

    _pool_block(u_ref, w_ref, pw_ref, ps_ref, o_ref, ext_ref, si, t_blk)


def _pool_halo(ext_ref, t_blk, first):
    if first:
        ext_ref[0:POOL_HALO, :] = jnp.zeros((POOL_HALO, D_POOL), F32)
    else:
        ext_ref[0:POOL_HALO, :] = ext_ref[t_blk:t_blk + POOL_HALO, :]


def _pool_block(u_ref, w_ref, pw_ref, ps_ref, o_ref, ext_ref, si, t_blk):
    proj = jnp.dot(u_ref[0], w_ref[...], preferred_element_type=F32)
    ext_ref[POOL_HALO:POOL_HALO + t_blk, :] = proj[:, :D_POOL]

    pos = (si * t_blk + 1 + lax.broadcasted_iota(jnp.int32, (t_blk, POOL_CH), 0)).astype(F32)
    for gi, win in enumerate(POOL_WINDOWS):
        lo = gi * POOL_CH
        acc = ext_ref[:, lo:lo + POOL_CH]
        shift = 1
        while shift < win:
            acc = acc + pltpu.roll(acc, shift, 0)
            shift *= 2
        cur = proj[:, lo:lo + POOL_CH]
        pooled = acc[POOL_HALO:, :] / jnp.minimum(pos, float(win)) - cur
        y = _dot(pooled, pw_ref[gi]) * ps_ref[:, lo:lo + POOL_CH]
        gate = proj[:, D_POOL + lo:D_POOL + lo + POOL_CH]
        o_ref[0, :, lo:lo + POOL_CH] = (y * _silu(gate)).astype(o_ref.dtype)


def _pool_call(u, w_pool, layer, pool_w, pool_scale, t_blk):
    b, s, d = u.shape
    return pl.pallas_call(
        functools.partial(_pool_kernel, t_blk=t_blk),
        grid=(b, s // t_blk),
        in_specs=[pl.BlockSpec((1, t_blk, d), lambda bi, si: (bi, si, 0)),
                  pl.BlockSpec((None, d, 2 * D_POOL), lambda bi, si: (layer, 0, 0)),
                  pl.BlockSpec((len(POOL_WINDOWS), POOL_CH, POOL_CH), lambda bi, si: (0, 0, 0)),
                  pl.BlockSpec((1, D_POOL), lambda bi, si: (0, 0))],
        out_specs=pl.BlockSpec((1, t_blk, D_POOL), lambda bi, si: (bi, si, 0)),
        out_shape=jax.ShapeDtypeStruct((b, s, D_POOL), BF16),
        scratch_shapes=[pltpu.VMEM((t_blk + POOL_HALO, D_POOL), F32)],
        compiler_params=pltpu.CompilerParams(
            dimension_semantics=("parallel", "arbitrary"),
            vmem_limit_bytes=VMEM_LIMIT),
        name="pool_mixer",
    )(u, w_pool, pool_w, pool_scale)


def _ssd_kernel(u_ref, w_ref, cw_ref, cb_ref, dtb_ref, alogc_ref, dskip_ref, nw_ref,
                exp_ref, btril_ref, dup_ref, o_ref,
                ext_ref, act_ref, z_ref, xdt_ref, cume_ref, cumt_ref, state_ref, *, t_blk, unroll):
    si = pl.program_id(1)
    n_chunks = t_blk // CHUNK

    @pl.when(si == 0)
    def _():
        ext_ref[0:CONV_HALO, :] = jnp.zeros((CONV_HALO, SSD_XBC_P), F32)
        state_ref[...] = jnp.zeros_like(state_ref)

    @pl.when(si > 0)
    def _():
        ext_ref[0:CONV_HALO, :] = ext_ref[t_blk:t_blk + CONV_HALO, :]

    proj = jnp.dot(u_ref[0], w_ref[...], preferred_element_type=F32)
    ext_ref[CONV_HALO:CONV_HALO + t_blk, :] = proj[:, :SSD_XBC_P]
    z_ref[...] = proj[:, SSD_XBC_P:SSD_XBC_P + D_SSD_P]

    assert SSD_CONV == 4
    e = ext_ref[...]
    e2 = pltpu.roll(e, 2, 0)
    even = cw_ref[3:4, :] * e + cw_ref[1:2, :] * e2
    odd = cw_ref[2:3, :] * e + cw_ref[0:1, :] * e2
    xc = (even + pltpu.roll(odd, 1, 0))[CONV_HALO:, :] + cb_ref[...]
    xc = _silu(xc)
    act_ref[...] = xc

    dt = _softplus(proj[:, SSD_XBC_P + D_SSD_P:] + dtb_ref[...])
    exp3 = jnp.concatenate([exp_ref[...]] * 3, axis=0)
    xdt_ref[...] = xc[:, :D_SSD_P] * _sel_right3(dt, exp3)

    dta = dt * (-jnp.exp(alogc_ref[...]) * LOG2E)
    sub = btril_ref.shape[0]
    dup3 = jnp.concatenate([dup_ref[...]] * 3, axis=0)
    for r0 in range(0, t_blk, sub):
        cum_c = _sel_left(btril_ref[...], dta[r0:r0 + sub, :])
        hi, mid, lo = _split3(cum_c)
        cume_ref[r0:r0 + sub, :] = jnp.dot(jnp.concatenate([hi, mid, lo], axis=1), exp3,
                                           preferred_element_type=F32)
        cum_t = lax.dot_general(jnp.concatenate([hi, mid, lo], axis=0), dup3,
                                (((0,), (0,)), ((), ())), preferred_element_type=F32)
        for c in range(sub // CHUNK):
            cumt_ref[r0 // CHUNK + c] = cum_t[:, c * LANES:(c + 1) * LANES]

    r64 = lax.broadcasted_iota(jnp.int32, (CHUNK, CHUNK), 0)
    c64 = lax.broadcasted_iota(jnp.int32, (CHUNK, CHUNK), 1)
    causal = r64 >= c64
    r128 = lax.broadcasted_iota(jnp.int32, (CHUNK, LANES), 0)
    l128 = lax.broadcasted_iota(jnp.int32, (CHUNK, LANES), 1)
    causal2 = r128 >= (l128 & (CHUNK - 1))
    first_half = lax.broadcasted_iota(jnp.int32, (1, LANES), 1) < CHUNK
    rb = lax.broadcasted_iota(jnp.int32, (LANES, LANES), 0)
    lb = lax.broadcasted_iota(jnp.int32, (LANES, LANES), 1)
    blockdiag = (rb >= CHUNK) == (lb >= CHUNK)
    b_off = D_SSD_P
    c_off = D_SSD_P + SSD_GROUPS * SSD_STATE
    groups = range(SSD_GROUPS)

    gl = [slice(g * SSD_GWP, (g + 1) * SSD_GWP) for g in groups]
    bl = [slice(b_off + g * SSD_STATE, b_off + (g + 1) * SSD_STATE) for g in groups]
    cl = [slice(c_off + g * SSD_STATE, c_off + (g + 1) * SSD_STATE) for g in groups]

    def trip(c, carry):
        chains = [(j, g) for j in range(unroll) for g in groups]
        rows = [_chunk_rows(c * unroll + j) for j in range(unroll)]
        cum_t = [cumt_ref[c * unroll + j] for j in range(unroll)]

        cb2 = {}
        for j, g in chains:
            bm = act_ref[rows[j], bl[g]]
            cb2[j, g] = _dot_nt(act_ref[rows[j], cl[g]], jnp.concatenate([bm, bm], axis=0))
        y_diag = {}
        for j, g in chains:
            hd = g * SSD_HPG
            cs01 = jnp.where(first_half, cum_t[j][hd:hd + 1, :], cum_t[j][hd + 1:hd + 2, :])
            ct01 = cume_ref[rows[j], g * SSD_GWP:g * SSD_GWP + LANES]
            seg01 = jnp.exp2(jnp.where(causal2, ct01 - cs01, -jnp.inf))
            ct2 = cume_ref[rows[j], g * SSD_GWP + LANES:g * SSD_GWP + LANES + CHUNK]
            seg2 = jnp.exp2(jnp.where(causal, ct2 - cum_t[j][hd + 2:hd + 3, 0:CHUNK], -jnp.inf))
            x01 = xdt_ref[rows[j], g * SSD_GWP:g * SSD_GWP + LANES]
            xbd = jnp.where(blockdiag, jnp.concatenate([x01, x01], axis=0), 0.0)
            y01 = _dot(cb2[j, g] * seg01, xbd)
            y2 = _dot(cb2[j, g][:, :CHUNK] * seg2, xdt_ref[rows[j], g * SSD_GWP + LANES:(g + 1) * SSD_GWP])
            y_diag[j, g] = jnp.concatenate([y01, y2], axis=1)

        y_off = {}
        for j, g in chains:
            cum_e = cume_ref[rows[j], gl[g]]
            last_e = cum_e[CHUNK - 1:CHUNK, :]
            hst = state_ref[g]
            y_off[j, g] = _dot(act_ref[rows[j], cl[g]], hst) * jnp.exp2(cum_e)
            xdec = xdt_ref[rows[j], gl[g]] * jnp.exp2(last_e - cum_e)
            state_ref[g] = hst * jnp.exp2(last_e) + _dot_tn(act_ref[rows[j], bl[g]], xdec)

        for j, g in chains:
            y = y_diag[j, g] + y_off[j, g] + act_ref[rows[j], gl[g]] * dskip_ref[:, gl[g]]
            y = y * _silu(z_ref[rows[j], gl[g]])
            ms = jnp.sum(y * y, axis=-1, keepdims=True) * (1.0 / SSD_GW)
            y = y * lax.rsqrt(ms + EPS) * nw_ref[:, gl[g]]
            o_ref[0, rows[j], gl[g]] = y.astype(o_ref.dtype)
        return carry

    _run_trips(n_chunks // unroll, trip)


def _ssd_call(u, w_ssd, layer, conv_w, conv_b, dt_bias, alog_c, dskip_e, nw_e, t_blk):
    b, s, d = u.shape
    n_chunks = t_blk // CHUNK
    heads = jnp.arange(SSD_HEADS)
    lanes_e = (heads // SSD_HPG) * SSD_GWP + (heads % SSD_HPG) * SSD_HEAD_DIM
    col_e = jnp.arange(D_SSD_P)[None, :]
    expand = ((col_e >= lanes_e[:, None]) & (col_e < lanes_e[:, None] + SSD_HEAD_DIM))
    expand = jnp.pad(expand, ((0, LANES - SSD_HEADS), (0, 0))).astype(BF16)
    sub = math.gcd(t_blk, SSD_SEL_ROWS)
    ti = jnp.arange(sub)
    btril = ((ti[:, None] >= ti[None, :]) & (ti[:, None] // CHUNK == ti[None, :] // CHUNK)).astype(BF16)
    cj = jnp.arange(2 * sub)
    dup = (ti[:, None] == (cj[None, :] // LANES) * CHUNK + cj[None, :] % CHUNK).astype(BF16)

    full = lambda shape: pl.BlockSpec(shape, lambda bi, si: (0,) * len(shape))
    return pl.pallas_call(
        functools.partial(_ssd_kernel, t_blk=t_blk, unroll=math.gcd(n_chunks, SSD_UNROLL)),
        grid=(b, s // t_blk),
        in_specs=[pl.BlockSpec((1, t_blk, d), lambda bi, si: (bi, si, 0)),
                  pl.BlockSpec((None, d, SSD_COLS), lambda bi, si: (layer, 0, 0)),
                  full((SSD_CONV, SSD_XBC_P)), full((1, SSD_XBC_P)),
                  full((1, LANES)), full((1, LANES)),
                  full((1, D_SSD_P)), full((1, D_SSD_P)),
                  full((LANES, D_SSD_P)), full((sub, sub)), full((sub, 2 * sub))],
        out_specs=pl.BlockSpec((1, t_blk, D_SSD_P), lambda bi, si: (bi, si, 0)),
        out_shape=jax.ShapeDtypeStruct((b, s, D_SSD_P), BF16),
        scratch_shapes=[pltpu.VMEM((t_blk + CONV_HALO, SSD_XBC_P), F32),
                        pltpu.VMEM((t_blk, SSD_XBC_P), F32),
                        pltpu.VMEM((t_blk, D_SSD_P), F32),
                        pltpu.VMEM((t_blk, D_SSD_P), F32),
                        pltpu.VMEM((t_blk, D_SSD_P), F32),
                        pltpu.VMEM((n_chunks, LANES, LANES), F32),
                        pltpu.VMEM((SSD_GROUPS, SSD_STATE, SSD_GWP), F32)],
        compiler_params=pltpu.CompilerParams(
            dimension_semantics=("parallel", "arbitrary"),
            vmem_limit_bytes=VMEM_LIMIT),
        name="ssd_mixer",
    )(u, w_ssd, conv_w, conv_b, dt_bias, alog_c, dskip_e, nw_e, expand, btril, dup)


def _out_kernel(h_ref, ohg_ref, opool_ref, ossd_ref, p_ref, w1_ref, w2_ref, w3_ref, wpg_ref, wpe_ref,
                nw_ref, *out_refs, emit_h):
    dot = lambda a, w: jnp.dot(a, w, preferred_element_type=F32)
    acc = h_ref[...] + dot(ohg_ref[...], w1_ref[...]) + dot(opool_ref[...], w2_ref[...]) \
        + dot(ossd_ref[...], w3_ref[...])
    gate = _sigmoid(_dot(acc, wpg_ref[...]))
    hn = acc + gate * _dot(p_ref[...], wpe_ref[...])
    y = hn * lax.rsqrt(jnp.mean(hn * hn, axis=-1, keepdims=True) + EPS) * nw_ref[...]
    if emit_h:
        out_refs[0][...] = hn
        out_refs[1][...] = y.astype(out_refs[1].dtype)
    else:
        out_refs[0][...] = y


def _out_call(h2d, ohg, opool, ossd, p2d, w1, w2, w3, wpg, wpe, layer, nw, tm, emit_h):
    n, d = h2d.shape
    rows = lambda width: pl.BlockSpec((tm, width), lambda i: (i, 0))
    full = lambda a: pl.BlockSpec((None,) + a.shape[1:], lambda i: (layer, 0, 0),
                                  pipeline_mode=pl.Buffered(1))
    if emit_h:
        out_shape = (jax.ShapeDtypeStruct((n, d), F32), jax.ShapeDtypeStruct((n, d), BF16))
        out_specs = (rows(d), rows(d))
    else:
        out_shape = jax.ShapeDtypeStruct((n, d), F32)
        out_specs = rows(d)
    return pl.pallas_call(
        functools.partial(_out_kernel, emit_h=emit_h),
        grid=(n // tm,),
        in_specs=[rows(d), rows(D_HG), rows(D_POOL), rows(D_SSD_P),
                  pl.BlockSpec((None, tm, P_DIM), lambda i: (layer, i, 0)),
                  full(w1), full(w2), full(w3), full(wpg), full(wpe), pl.BlockSpec((1, d), lambda i: (0, 0))],
        out_specs=out_specs,
        out_shape=out_shape,
        compiler_params=pltpu.CompilerParams(
            dimension_semantics=("parallel",), vmem_limit_bytes=VMEM_LIMIT),
        name="out_proj",
    )(h2d, ohg, opool, ossd, p2d, w1, w2, w3, wpg, wpe, nw.reshape(1, d))


def _pad_groups(a):
    lead = a.shape[:-1]
    a = a.reshape(lead + (SSD_GROUPS, SSD_GW))
    a = jnp.pad(a, [(0, 0)] * len(lead) + [(0, 0), (0, SSD_GWP - SSD_GW)])
    return a.reshape(lead + (D_SSD_P,))


def _pad_lanes(a):
    return jnp.pad(a, [(0, 0)] * (a.ndim - 1) + [(0, LANES - a.shape[-1])])


C_POOL = 4 * D_HG
C_XBC = C_POOL + 2 * D_POOL
C_B = C_XBC + D_SSD
C_C = C_B + SSD_GROUPS * SSD_STATE
C_DT = C_C + SSD_GROUPS * SSD_STATE
C_Z = C_DT + SSD_HEADS
N_IN = C_Z + D_SSD
N_IN_MAIN = (N_IN // LANES) * LANES
WPREP_ROWS = 256


def _wprep_kernel(w_ref, tail_ref, hg_ref, pool_ref, ssd_ref):
    rows = w_ref.shape[1]
    lane = lax.broadcasted_iota(jnp.int32, (rows, LANES), 1)
    tail = jnp.where(lane < N_IN - N_IN_MAIN, tail_ref[0], 0.0)
    hg_ref[0] = w_ref[0, :, 0:C_POOL].astype(BF16)
    pool_ref[0] = w_ref[0, :, C_POOL:C_XBC].astype(BF16)

    def window(a0, width):
        if a0 + width <= N_IN_MAIN:
            return w_ref[0, :, a0:a0 + width]
        assert a0 + width == N_IN_MAIN + LANES
        return jnp.concatenate([w_ref[0, :, a0:N_IN_MAIN], tail], axis=1)

    lane_g = lax.broadcasted_iota(jnp.int32, (rows, SSD_GWP), 1)

    def group(a):
        a0 = (a // LANES) * LANES
        win = window(a0, SSD_GWP + LANES)
        if a != a0:
            win = pltpu.roll(win, SSD_GWP + LANES - (a - a0), 1)
        return jnp.where(lane_g < SSD_GW, win[:, :SSD_GWP], 0.0)

    for g in range(SSD_GROUPS):
        ssd_ref[0, :, g * SSD_GWP:(g + 1) * SSD_GWP] = group(C_XBC + g * SSD_GW).astype(BF16)
        zo = SSD_XBC_P + g * SSD_GWP
        ssd_ref[0, :, zo:zo + SSD_GWP] = group(C_Z + g * SSD_GW).astype(BF16)
    ssd_ref[0, :, D_SSD_P:SSD_XBC_P] = w_ref[0, :, C_B:C_DT].astype(BF16)
    dt = jnp.where(lane < SSD_HEADS, window(C_DT, LANES), 0.0)
    ssd_ref[0, :, SSD_XBC_P + D_SSD_P:] = dt.astype(BF16)


def _wprep_call(w_in):
    depth, d, n_in = w_in.shape
    assert n_in == N_IN and C_DT % LANES == 0 and d % WPREP_ROWS == 0
    out = lambda width: pl.BlockSpec((1, WPREP_ROWS, width), lambda i, r: (i, r, 0))
    return pl.pallas_call(
        _wprep_kernel,
        grid=(depth, d // WPREP_ROWS),
        in_specs=[pl.BlockSpec((1, WPREP_ROWS, N_IN_MAIN), lambda i, r: (i, r, 0)),
                  pl.BlockSpec((1, WPREP_ROWS, LANES), lambda i, r: (i, r, N_IN_MAIN // LANES))],
        out_specs=(out(C_POOL), out(C_XBC - C_POOL), out(SSD_COLS)),
        out_shape=(jax.ShapeDtypeStruct((depth, d, C_POOL), BF16),
                   jax.ShapeDtypeStruct((depth, d, C_XBC - C_POOL), BF16),
                   jax.ShapeDtypeStruct((depth, d, SSD_COLS), BF16)),
        compiler_params=pltpu.CompilerParams(
            dimension_semantics=("parallel", "parallel"), vmem_limit_bytes=VMEM_LIMIT),
        name="weight_prep",
    )(w_in, w_in)


def _wcast_kernel(wo_ref, wpg_ref, wpe_ref, o1_ref, o2_ref, o3_ref, opg_ref, ope_ref):
    o1_ref[0] = wo_ref[0, 0:D_HG, :].astype(BF16)
    o2_ref[0] = wo_ref[0, D_HG:D_HG + D_POOL, :].astype(BF16)
    base = D_HG + D_POOL
    for g in range(SSD_GROUPS):
        o3_ref[0, g * SSD_GWP:g * SSD_GWP + SSD_GW, :] = \
            wo_ref[0, base + g * SSD_GW:base + (g + 1) * SSD_GW, :].astype(BF16)
        o3_ref[0, g * SSD_GWP + SSD_GW:(g + 1) * SSD_GWP, :] = jnp.zeros((SSD_GWP - SSD_GW, wo_ref.shape[2]), BF16)
    opg_ref[0] = wpg_ref[0].astype(BF16)
    ope_ref[0] = wpe_ref[0].astype(BF16)


def _wcast_call(w_out, w_pg, w_pe):
    depth, d_mix, d = w_out.shape
    assert d_mix == D_HG + D_POOL + D_SSD
    layer = lambda a: pl.BlockSpec((1,) + a.shape[1:], lambda i: (i, 0, 0))
    shapes = [(depth, D_HG, d), (depth, D_POOL, d), (depth, D_SSD_P, d), w_pg.shape, w_pe.shape]
    outs = [jax.ShapeDtypeStruct(sh, BF16) for sh in shapes]
    return pl.pallas_call(
        _wcast_kernel,
        grid=(depth,),
        in_specs=[layer(w_out), layer(w_pg), layer(w_pe)],
        out_specs=tuple(layer(o) for o in outs),
        out_shape=tuple(outs),
        compiler_params=pltpu.CompilerParams(dimension_semantics=("parallel",), vmem_limit_bytes=VMEM_LIMIT),
        name="weight_cast",
    )(w_out, w_pg, w_pe)


def kernel(x, p, norm_w, w_in, hg_lb, hg_norm_w, pool_w, pool_scale, conv_w, conv_b, dt_bias, a_log, d_skip,
           ssd_norm_w, w_out, w_pe, w_pg, final_norm_w):
    b, s, d = x.shape
    depth = w_in.shape[0]
    n = b * s
    t_hg = min(2048, s)
    t_pool = min(1024, s)
    t_ssd = min(512, s)
    tm = min(1024, n)

    lb_all = jnp.cumsum(jax.nn.softmax(hg_lb.astype(F32), axis=0), axis=0)
    lb_all = lb_all - lb_all[0]


    w_hg_all, w_pool_all, w_ssd_all = _wprep_call(w_in)
    w_out_parts = _wcast_call(w_out, w_pg, w_pe)

    h = x.reshape(n, d)
    u = _rms_norm_call(h, norm_w[0], tm).reshape(b, s, d)
    out = None
    for i in range(depth):
        lb = lb_all[i].reshape(1, D_HG)
        o_hg = _hgrn_call(u, w_hg_all, i, jnp.log(lb) * LOG2E, jnp.log1p(-lb) * LOG2E,
                          hg_norm_w[i].reshape(1, D_HG), t_hg)
        o_pool = _pool_call(u, w_pool_all, i, pool_w[i].astype(BF16),
                            pool_scale[i].reshape(1, D_POOL), t_pool)
        xbc_split = lambda a: jnp.concatenate([_pad_groups(a[..., :D_SSD]), a[..., D_SSD:]], axis=-1)
        rep = lambda a: _pad_groups(jnp.repeat(a, SSD_HEAD_DIM)).reshape(1, D_SSD_P)
        o_ssd = _ssd_call(u, w_ssd_all, i, xbc_split(conv_w[i]), xbc_split(conv_b[i]).reshape(1, SSD_XBC_P),
                          _pad_lanes(dt_bias[i]).reshape(1, LANES), _pad_lanes(a_log[i]).reshape(1, LANES),
                          rep(d_skip[i]), _pad_groups(ssd_norm_w[i]).reshape(1, D_SSD_P), t_ssd)
        last = i == depth - 1
        nw_next = final_norm_w if last else norm_w[i + 1]
        res = _out_call(h, o_hg.reshape(n, D_HG), o_pool.reshape(n, D_POOL), o_ssd.reshape(n, D_SSD_P),
                        p.reshape(depth, n, P_DIM), *w_out_parts, i, nw_next, tm, emit_h=not last)
        if last:
            out = res
        else:
            h, u = res
            u = u.reshape(b, s, d)
    return out.reshape(b, s, d)
```

```python
import functools
import math

import jax
import jax.numpy as jnp
from jax import lax
from jax.experimental import pallas as pl
from jax.experimental.pallas import tpu as pltpu

F32 = jnp.float32
BF16 = jnp.bfloat16

CHUNK = 64
EPS = 1e-6
D_MODEL = 1024
P_DIM = 256
HG_HEADS = 6
HG_D = 128
HG_PAIR = 2
HG_LEVELS = (64, 32, 16, 8, 4)
HG_HEAD_ROWS = 256
HG_GROUP = 16
HG_UNROLL = 32
LOG2E = 1.4426950408889634
D_HG = HG_HEADS * HG_D
POOL_WINDOWS = (2, 4, 8, 16)
POOL_CH = 128
D_POOL = len(POOL_WINDOWS) * POOL_CH
POOL_HALO = 16
SSD_HEADS = 12
SSD_HEAD_DIM = 64
SSD_GROUPS = 4
SSD_HPG = SSD_HEADS // SSD_GROUPS
SSD_GW = SSD_HPG * SSD_HEAD_DIM
SSD_GWP = 256
D_SSD = SSD_HEADS * SSD_HEAD_DIM
D_SSD_P = SSD_GROUPS * SSD_GWP
SSD_STATE = 128
SSD_CONV = 4
CONV_HALO = 8
SSD_STAGE_CHUNKS = 4
SSD_SEL_ROWS = 256
SSD_UNROLL = 8
LANES = 128
SSD_XBC_P = D_SSD_P + 2 * SSD_GROUPS * SSD_STATE
SSD_COLS = SSD_XBC_P + D_SSD_P + LANES
VMEM_LIMIT = 48 * 1024 * 1024


def _dot(a, b):
    return jnp.dot(a.astype(BF16), b.astype(BF16), preferred_element_type=F32)


def _dot_nt(a, b):
    return lax.dot_general(a.astype(BF16), b.astype(BF16), (((1,), (1,)), ((), ())),
                           preferred_element_type=F32)


def _dot_tn(a, b):
    return lax.dot_general(a.astype(BF16), b.astype(BF16), (((0,), (0,)), ((), ())),
                           preferred_element_type=F32)


def _split3(x):
    hi = x.astype(BF16)
    r1 = x - hi.astype(F32)
    mid = r1.astype(BF16)
    lo = (r1 - mid.astype(F32)).astype(BF16)
    return hi, mid, lo


def _sel_left(m, x):
    hi, mid, lo = _split3(x)
    d = lambda p: jnp.dot(m, p, preferred_element_type=F32)
    return d(hi) + (d(mid) + d(lo))


def _sel_left3(m3, x):
    return jnp.dot(m3, jnp.concatenate(_split3(x), axis=0), preferred_element_type=F32)


def _sel_right3(x, m3):
    return jnp.dot(jnp.concatenate(_split3(x), axis=1), m3, preferred_element_type=F32)


def _chunk_rows(i):
    if isinstance(i, int):
        return pl.ds(i * CHUNK, CHUNK)
    return pl.ds(pl.multiple_of(i * CHUNK, CHUNK), CHUNK)


def _run_trips(n_trips, body):
    if n_trips == 1:
        body(0, 0)
    else:
        lax.fori_loop(0, n_trips, body, 0)


def _sigmoid(x):
    return 1.0 / (1.0 + jnp.exp(-x))


def _silu(x):
    return x * _sigmoid(x)


def _softplus(x):
    return jnp.maximum(x, 0.0) + jnp.log1p(jnp.exp(-jnp.abs(x)))


def _norm_kernel(x_ref, w_ref, o_ref):
    x = x_ref[...]
    y = x * lax.rsqrt(jnp.mean(x * x, axis=-1, keepdims=True) + EPS)
    o_ref[...] = (y * w_ref[...]).astype(o_ref.dtype)


def _rms_norm_call(x2d, w, tm):
    n, d = x2d.shape
    return pl.pallas_call(
        _norm_kernel,
        grid=(n // tm,),
        in_specs=[pl.BlockSpec((tm, d), lambda i: (i, 0)),
                  pl.BlockSpec((1, d), lambda i: (0, 0))],
        out_specs=pl.BlockSpec((tm, d), lambda i: (i, 0)),
        out_shape=jax.ShapeDtypeStruct((n, d), BF16),
        compiler_params=pltpu.CompilerParams(dimension_semantics=("parallel",)),
        name="rms_norm_in",
    )(x2d, w.reshape(1, d))


def _hgrn_kernel(u_ref, wq_ref, wf_ref, wv_ref, wg_ref, loglb_ref, log1mlb_ref, nw_ref, o_ref,
                 proj_ref, vg_ref, cum_ref, pair_ref, state_ref, *, n_chunks, unroll):
    @pl.when(pl.program_id(2) == 0)
    def _():
        state_ref[...] = jnp.zeros_like(state_ref)

    pw = HG_PAIR * HG_D
    w_refs = (wq_ref, wf_ref, wv_ref, wg_ref)

    def project(kind, r0, r1):
        dst, slot = (proj_ref, kind) if kind < 2 else (vg_ref, kind - 2)
        dst[r0:r1, slot * pw:(slot + 1) * pw] = jnp.dot(u_ref[0, r0:r1, :], w_refs[kind][...],
                                                        preferred_element_type=F32)

    t_blk = n_chunks * CHUNK
    for r0 in range(0, t_blk, HG_HEAD_ROWS):
        project(1, r0, min(r0 + HG_HEAD_ROWS, t_blk))
    for kind in (0, 2, 3):
        project(kind, 0, t_blk)

    row = lax.broadcasted_iota(jnp.int32, (CHUNK, HG_D), 0)
    sub = lax.broadcasted_iota(jnp.int32, (CHUNK // 8, 8, HG_D), 1)
    r64 = lax.broadcasted_iota(jnp.int32, (CHUNK, CHUNK), 0)
    c64 = lax.broadcasted_iota(jnp.int32, (CHUNK, CHUNK), 1)
    r3 = lax.broadcasted_iota(jnp.int32, (CHUNK, 3 * CHUNK), 0)
    c3 = lax.broadcasted_iota(jnp.int32, (CHUNK, 3 * CHUNK), 1)
    tril3 = (r3 >= (c3 % CHUNK)).astype(BF16)
    level_mask = {bs: ((r64 // bs) == (c64 // bs)) & ((r64 & (bs // 2)) != 0) & ((c64 & (bs // 2)) == 0)
                  for bs in HG_LEVELS}
    level_sign = {bs: jnp.where((row & (bs // 2)) != 0, 1.0, -1.0) for bs in HG_LEVELS}
    diag0 = r64 == c64
    diag1 = (r64 == c64 + 1) & ((r64 & 1) == 1)
    odd = (row & 1) == 1

    def ref_rows(cum, bs):
        half = bs // 2
        if bs >= 8:
            c3 = cum.reshape(CHUNK // bs, bs, HG_D)
            return jnp.broadcast_to(c3[:, half - 1:half, :], c3.shape).reshape(CHUNK, HG_D)
        c3 = cum.reshape(CHUNK // 8, 8, HG_D)
        lo = jnp.broadcast_to(c3[:, 1:2, :], c3.shape)
        hi = jnp.broadcast_to(c3[:, 5:6, :], c3.shape)
        return jnp.where(sub < 4, lo, hi).reshape(CHUNK, HG_D)

    def gates(c, carry):
        rows = _chunk_rows(c)
        for hh in range(HG_PAIR):
            lanes = slice(hh * HG_D, (hh + 1) * HG_D)
            z = proj_ref[rows, pw + hh * HG_D:pw + (hh + 1) * HG_D]
            loglb = loglb_ref[:, lanes]
            log1mlb = log1mlb_ref[:, lanes]
            zs = z * LOG2E
            log_sig = jnp.minimum(zs, 0.0) - jnp.log2(1.0 + jnp.exp2(-jnp.abs(zs)))
            b_ = log1mlb + log_sig
            log_f = jnp.maximum(loglb, b_) + jnp.log2(1.0 + jnp.exp2(-jnp.abs(loglb - b_)))
            f = jnp.exp2(log_f)
            kk = 1.0 - f
            proj_ref[rows, pw + hh * HG_D:pw + (hh + 1) * HG_D] = kk
            cum_ref[rows, lanes] = _sel_left3(tril3, log_f)
            pair_ref[rows, lanes] = jnp.where(odd, pltpu.roll(kk, 1, 0) * f, 0.0)
        return carry

    _run_trips(n_chunks // unroll, lambda c, carry: [gates(c * unroll + j, carry) for j in range(unroll)][-1])

    def trip(c, carry):
        for j0 in range(0, unroll, min(unroll, HG_GROUP)):
            stage_group(c, range(j0, j0 + min(unroll, HG_GROUP)))
        return carry

    def stage_group(c, chunk_ids):
        chains = [(j, hh) for j in chunk_ids for hh in range(HG_PAIR)]
        rows = {j: _chunk_rows(c * unroll + j) for j in chunk_ids}
        lanes = [slice(hh * HG_D, (hh + 1) * HG_D) for hh in range(HG_PAIR)]
        col = lambda kind, hh: slice(kind * pw + hh * HG_D, kind * pw + (hh + 1) * HG_D)
        q = {ch: proj_ref[rows[ch[0]], col(0, ch[1])] for ch in chains}
        kk = {ch: proj_ref[rows[ch[0]], col(1, ch[1])] for ch in chains}
        cum = {ch: cum_ref[rows[ch[0]], lanes[ch[1]]] for ch in chains}

        scores = {}
        for ch in chains:
            acc = None
            for bs in HG_LEVELS:
                e = jnp.exp2((cum[ch] - ref_rows(cum[ch], bs)) * level_sign[bs])
                m = jnp.where(level_mask[bs], _dot_nt(q[ch] * e, kk[ch] * e), 0.0)
                acc = m if acc is None else acc + m
            rs0 = jnp.sum(q[ch] * kk[ch], axis=1, keepdims=True)
            rs1 = jnp.sum(q[ch] * pair_ref[rows[ch[0]], lanes[ch[1]]], axis=1, keepdims=True)
            scores[ch] = acc + jnp.where(diag0, rs0, 0.0) + jnp.where(diag1, rs1, 0.0)

        inter = {}
        for ch in chains:
            j, hh = ch
            last = cum[ch][CHUNK - 1:CHUNK, :]
            st = state_ref[hh]
            inter[ch] = _dot_nt(q[ch] * jnp.exp2(cum[ch]), st)
            v = vg_ref[rows[j], col(0, hh)]
            state_ref[hh] = st * jnp.exp2(last) + _dot_tn(v, kk[ch] * jnp.exp2(last - cum[ch]))

        for ch in chains:
            j, hh = ch
            o = inter[ch] + _dot(scores[ch], vg_ref[rows[j], col(0, hh)])
            o = o * lax.rsqrt(jnp.mean(o * o, axis=-1, keepdims=True) + EPS) * nw_ref[:, lanes[hh]]
            g = vg_ref[rows[j], col(1, hh)]
            o_ref[0, rows[j], lanes[hh]] = (o * _silu(g)).astype(o_ref.dtype)

    _run_trips(n_chunks // unroll, trip)


def _hgrn_call(u, w_hg, layer, loglb, log1mlb, nw, t_blk):
    b, s, d = u.shape
    n_pairs = HG_HEADS // HG_PAIR
    pw = HG_PAIR * HG_D
    grid = (b, n_pairs, s // t_blk)
    vec = pl.BlockSpec((1, pw), lambda bi, hp, si: (0, hp))
    wspec = lambda kind: pl.BlockSpec((None, d, pw), lambda bi, hp, si: (layer, 0, kind * n_pairs + hp))
    return pl.pallas_call(
        functools.partial(_hgrn_kernel, n_chunks=t_blk // CHUNK, unroll=math.gcd(t_blk // CHUNK, HG_UNROLL)),
        grid=grid,
        in_specs=[pl.BlockSpec((1, t_blk, d), lambda bi, hp, si: (bi, si, 0)),
                  wspec(0), wspec(1), wspec(2), wspec(3),
                  vec, vec, vec],
        out_specs=pl.BlockSpec((1, t_blk, pw), lambda bi, hp, si: (bi, si, hp)),
        out_shape=jax.ShapeDtypeStruct((b, s, D_HG), BF16),
        scratch_shapes=[pltpu.VMEM((t_blk, 2 * pw), F32),
                        pltpu.VMEM((t_blk, 2 * pw), F32),
                        pltpu.VMEM((t_blk, pw), F32),
                        pltpu.VMEM((t_blk, pw), F32),
                        pltpu.VMEM((HG_PAIR, HG_D, HG_D), F32)],
        compiler_params=pltpu.CompilerParams(
            dimension_semantics=("parallel", "parallel", "arbitrary"),
            vmem_limit_bytes=VMEM_LIMIT),
        name="hgrn2_mixer",
    )(u, w_hg, w_hg, w_hg, w_hg, loglb, log1mlb, nw)


def _pool_kernel(u_ref, w_ref, pw_ref, ps_ref, o_ref, ext_ref, *, t_blk):
    si = pl.program_id(1)

    @pl.when(si == 0)
    def _():
        _pool_halo(ext_ref, t_blk, first=True)

    @pl.when(si > 0)
    def _():
        _pool_halo(ext_ref, t_blk, first=False)

    _pool_block(u_ref, w_ref, pw_ref, ps_ref, o_ref, ext_ref, si, t_blk)


def _pool_halo(ext_ref, t_blk, first):
    if first:
        ext_ref[0:POOL_HALO, :] = jnp.zeros((POOL_HALO, D_POOL), F32)
    else:
        ext_ref[0:POOL_HALO, :] = ext_ref[t_blk:t_blk + POOL_HALO, :]


def _pool_block(u_ref, w_ref, pw_ref, ps_ref, o_ref, ext_ref, si, t_blk):
    proj = jnp.dot(u_ref[0], w_ref[...], preferred_element_type=F32)
    ext_ref[POOL_HALO:POOL_HALO + t_blk, :] = proj[:, :D_POOL]

    pos = (si * t_blk + 1 + lax.broadcasted_iota(jnp.int32, (t_blk, POOL_CH), 0)).astype(F32)
    for gi, win in enumerate(POOL_WINDOWS):
        lo = gi * POOL_CH
        acc = ext_ref[:, lo:lo + POOL_CH]
        shift = 1
        while shift < win:
            acc = acc + pltpu.roll(acc, shift, 0)
            shift *= 2
        cur = proj[:, lo:lo + POOL_CH]
        pooled = acc[POOL_HALO:, :] / jnp.minimum(pos, float(win)) - cur
        y = _dot(pooled, pw_ref[gi]) * ps_ref[:, lo:lo + POOL_CH]
        gate = proj[:, D_POOL + lo:D_POOL + lo + POOL_CH]
        o_ref[0, :, lo:lo + POOL_CH] = (y * _silu(gate)).astype(o_ref.dtype)


def _pool_call(u, w_pool, layer, pool_w, pool_scale, t_blk):
    b, s, d = u.shape
    return pl.pallas_call(
        functools.partial(_pool_kernel, t_blk=t_blk),
        grid=(b, s // t_blk),
        in_specs=[pl.BlockSpec((1, t_blk, d), lambda bi, si: (bi, si, 0)),
                  pl.BlockSpec((None, d, 2 * D_POOL), lambda bi, si: (layer, 0, 0)),
                  pl.BlockSpec((len(POOL_WINDOWS), POOL_CH, POOL_CH), lambda bi, si: (0, 0, 0)),
                  pl.BlockSpec((1, D_POOL), lambda bi, si: (0, 0))],
        out_specs=pl.BlockSpec((1, t_blk, D_POOL), lambda bi, si: (bi, si, 0)),
        out_shape=jax.ShapeDtypeStruct((b, s, D_POOL), BF16),
        scratch_shapes=[pltpu.VMEM((t_blk + POOL_HALO, D_POOL), F32)],
        compiler_params=pltpu.CompilerParams(
            dimension_semantics=("parallel", "arbitrary"),
            vmem_limit_bytes=VMEM_LIMIT),
        name="pool_mixer",
    )(u, w_pool, pool_w, pool_scale)


def _ssd_kernel(u_ref, w_ref, cw_ref, cb_ref, dtb_ref, alogc_ref, dskip_ref, nw_ref,
                exp_ref, btril_ref, dup_ref, o_ref,
                ext_ref, act_ref, z_ref, xdt_ref, cume_ref, cumt_ref, state_ref, *, t_blk, unroll):
    si = pl.program_id(1)
    n_chunks = t_blk // CHUNK

    @pl.when(si == 0)
    def _():
        ext_ref[0:CONV_HALO, :] = jnp.zeros((CONV_HALO, SSD_XBC_P), F32)
        state_ref[...] = jnp.zeros_like(state_ref)

    @pl.when(si > 0)
    def _():
        ext_ref[0:CONV_HALO, :] = ext_ref[t_blk:t_blk + CONV_HALO, :]

    proj = jnp.dot(u_ref[0], w_ref[...], preferred_element_type=F32)
    ext_ref[CONV_HALO:CONV_HALO + t_blk, :] = proj[:, :SSD_XBC_P]
    z_ref[...] = proj[:, SSD_XBC_P:SSD_XBC_P + D_SSD_P]

    assert SSD_CONV == 4
    e = ext_ref[...]
    e2 = pltpu.roll(e, 2, 0)
    even = cw_ref[3:4, :] * e + cw_ref[1:2, :] * e2
    odd = cw_ref[2:3, :] * e + cw_ref[0:1, :] * e2
    xc = (even + pltpu.roll(odd, 1, 0))[CONV_HALO:, :] + cb_ref[...]
    xc = _silu(xc)
    act_ref[...] = xc

    dt = _softplus(proj[:, SSD_XBC_P + D_SSD_P:] + dtb_ref[...])
    exp3 = jnp.concatenate([exp_ref[...]] * 3, axis=0)
    xdt_ref[...] = xc[:, :D_SSD_P] * _sel_right3(dt, exp3)

    dta = dt * (-jnp.exp(alogc_ref[...]) * LOG2E)
    sub = btril_ref.shape[0]
    dup3 = jnp.concatenate([dup_ref[...]] * 3, axis=0)
    for r0 in range(0, t_blk, sub):
        cum_c = _sel_left(btril_ref[...], dta[r0:r0 + sub, :])
        hi, mid, lo = _split3(cum_c)
        cume_ref[r0:r0 + sub, :] = jnp.dot(jnp.concatenate([hi, mid, lo], axis=1), exp3,
                                           preferred_element_type=F32)
        cum_t = lax.dot_general(jnp.concatenate([hi, mid, lo], axis=0), dup3,
                                (((0,), (0,)), ((), ())), preferred_element_type=F32)
        for c in range(sub // CHUNK):
            cumt_ref[r0 // CHUNK + c] = cum_t[:, c * LANES:(c + 1) * LANES]

    r64 = lax.broadcasted_iota(jnp.int32, (CHUNK, CHUNK), 0)
    c64 = lax.broadcasted_iota(jnp.int32, (CHUNK, CHUNK), 1)
    causal = r64 >= c64
    r128 = lax.broadcasted_iota(jnp.int32, (CHUNK, LANES), 0)
    l128 = lax.broadcasted_iota(jnp.int32, (CHUNK, LANES), 1)
    causal2 = r128 >= (l128 & (CHUNK - 1))
    first_half = lax.broadcasted_iota(jnp.int32, (1, LANES), 1) < CHUNK
    rb = lax.broadcasted_iota(jnp.int32, (LANES, LANES), 0)
    lb = lax.broadcasted_iota(jnp.int32, (LANES, LANES), 1)
    blockdiag = (rb >= CHUNK) == (lb >= CHUNK)
    b_off = D_SSD_P
    c_off = D_SSD_P + SSD_GROUPS * SSD_STATE
    groups = range(SSD_GROUPS)

    gl = [slice(g * SSD_GWP, (g + 1) * SSD_GWP) for g in groups]
    bl = [slice(b_off + g * SSD_STATE, b_off + (g + 1) * SSD_STATE) for g in groups]
    cl = [slice(c_off + g * SSD_STATE, c_off + (g + 1) * SSD_STATE) for g in groups]

    def trip(c, carry):
        step = min(unroll, SSD_STAGE_CHUNKS)
        for j0 in range(0, unroll, step):
            stage_group(c, range(j0, j0 + step))
        return carry

    def stage_group(c, chunk_ids):
        chains = [(j, g) for j in chunk_ids for g in groups]
        rows = {j: _chunk_rows(c * unroll + j) for j in chunk_ids}
        cum_t = {j: cumt_ref[c * unroll + j] for j in chunk_ids}

        cb2 = {}
        for j, g in chains:
            bm = act_ref[rows[j], bl[g]]
            cb2[j, g] = _dot_nt(act_ref[rows[j], cl[g]], jnp.concatenate([bm, bm], axis=0))
        y_diag = {}
        for j, g in chains:
            hd = g * SSD_HPG
            cs01 = jnp.where(first_half, cum_t[j][hd:hd + 1, :], cum_t[j][hd + 1:hd + 2, :])
            ct01 = cume_ref[rows[j], g * SSD_GWP:g * SSD_GWP + LANES]
            seg01 = jnp.exp2(jnp.where(causal2, ct01 - cs01, -jnp.inf))
            ct2 = cume_ref[rows[j], g * SSD_GWP + LANES:g * SSD_GWP + LANES + CHUNK]
            seg2 = jnp.exp2(jnp.where(causal, ct2 - cum_t[j][hd + 2:hd + 3, 0:CHUNK], -jnp.inf))
            x01 = xdt_ref[rows[j], g * SSD_GWP:g * SSD_GWP + LANES]
            xbd = jnp.where(blockdiag, jnp.concatenate([x01, x01], axis=0), 0.0)
            y01 = _dot(cb2[j, g] * seg01, xbd)
            y2 = _dot(cb2[j, g][:, :CHUNK] * seg2, xdt_ref[rows[j], g * SSD_GWP + LANES:(g + 1) * SSD_GWP])
            y_diag[j, g] = jnp.concatenate([y01, y2], axis=1)

        y_off = {}
        for j, g in chains:
            cum_e = cume_ref[rows[j], gl[g]]
            last_e = cum_e[CHUNK - 1:CHUNK, :]
            hst = state_ref[g]
            y_off[j, g] = _dot(act_ref[rows[j], cl[g]], hst) * jnp.exp2(cum_e)
            xdec = xdt_ref[rows[j], gl[g]] * jnp.exp2(last_e - cum_e)
            state_ref[g] = hst * jnp.exp2(last_e) + _dot_tn(act_ref[rows[j], bl[g]], xdec)

        for j, g in chains:
            y = y_diag[j, g] + y_off[j, g] + act_ref[rows[j], gl[g]] * dskip_ref[:, gl[g]]
            y = y * _silu(z_ref[rows[j], gl[g]])
            ms = jnp.sum(y * y, axis=-1, keepdims=True) * (1.0 / SSD_GW)
            y = y * lax.rsqrt(ms + EPS) * nw_ref[:, gl[g]]
            o_ref[0, rows[j], gl[g]] = y.astype(o_ref.dtype)

    _run_trips(n_chunks // unroll, trip)


def _ssd_call(u, w_ssd, layer, conv_w, conv_b, dt_bias, alog_c, dskip_e, nw_e, t_blk):
    b, s, d = u.shape
    n_chunks = t_blk // CHUNK
    heads = jnp.arange(SSD_HEADS)
    lanes_e = (heads // SSD_HPG) * SSD_GWP + (heads % SSD_HPG) * SSD_HEAD_DIM
    col_e = jnp.arange(D_SSD_P)[None, :]
    expand = ((col_e >= lanes_e[:, None]) & (col_e < lanes_e[:, None] + SSD_HEAD_DIM))
    expand = jnp.pad(expand, ((0, LANES - SSD_HEADS), (0, 0))).astype(BF16)
    sub = math.gcd(t_blk, SSD_SEL_ROWS)
    ti = jnp.arange(sub)
    btril = ((ti[:, None] >= ti[None, :]) & (ti[:, None] // CHUNK == ti[None, :] // CHUNK)).astype(BF16)
    cj = jnp.arange(2 * sub)
    dup = (ti[:, None] == (cj[None, :] // LANES) * CHUNK + cj[None, :] % CHUNK).astype(BF16)

    full = lambda shape: pl.BlockSpec(shape, lambda bi, si: (0,) * len(shape))
    return pl.pallas_call(
        functools.partial(_ssd_kernel, t_blk=t_blk, unroll=math.gcd(n_chunks, SSD_UNROLL)),
        grid=(b, s // t_blk),
        in_specs=[pl.BlockSpec((1, t_blk, d), lambda bi, si: (bi, si, 0)),
                  pl.BlockSpec((None, d, SSD_COLS), lambda bi, si: (layer, 0, 0)),
                  full((SSD_CONV, SSD_XBC_P)), full((1, SSD_XBC_P)),
                  full((1, LANES)), full((1, LANES)),
                  full((1, D_SSD_P)), full((1, D_SSD_P)),
                  full((LANES, D_SSD_P)), full((sub, sub)), full((sub, 2 * sub))],
        out_specs=pl.BlockSpec((1, t_blk, D_SSD_P), lambda bi, si: (bi, si, 0)),
        out_shape=jax.ShapeDtypeStruct((b, s, D_SSD_P), BF16),
        scratch_shapes=[pltpu.VMEM((t_blk + CONV_HALO, SSD_XBC_P), F32),
                        pltpu.VMEM((t_blk, SSD_XBC_P), F32),
                        pltpu.VMEM((t_blk, D_SSD_P), F32),
                        pltpu.VMEM((t_blk, D_SSD_P), F32),
                        pltpu.VMEM((t_blk, D_SSD_P), F32),
                        pltpu.VMEM((n_chunks, LANES, LANES), F32),
                        pltpu.VMEM((SSD_GROUPS, SSD_STATE, SSD_GWP), F32)],
        compiler_params=pltpu.CompilerParams(
            dimension_semantics=("parallel", "arbitrary"),
            vmem_limit_bytes=VMEM_LIMIT),
        name="ssd_mixer",
    )(u, w_ssd, conv_w, conv_b, dt_bias, alog_c, dskip_e, nw_e, expand, btril, dup)


def _out_kernel(h_ref, ohg_ref, opool_ref, ossd_ref, p_ref, w1_ref, w2_ref, w3_ref, wpg_ref, wpe_ref,
                nw_ref, *out_refs, emit_h):
    dot = lambda a, w: jnp.dot(a, w, preferred_element_type=F32)
    acc = h_ref[...] + dot(ohg_ref[...], w1_ref[...]) + dot(opool_ref[...], w2_ref[...]) \
        + dot(ossd_ref[...], w3_ref[...])
    gate = _sigmoid(_dot(acc, wpg_ref[...]))
    hn = acc + gate * _dot(p_ref[...], wpe_ref[...])
    y = hn * lax.rsqrt(jnp.mean(hn * hn, axis=-1, keepdims=True) + EPS) * nw_ref[...]
    if emit_h:
        out_refs[0][...] = hn
        out_refs[1][...] = y.astype(out_refs[1].dtype)
    else:
        out_refs[0][...] = y


def _out_call(h2d, ohg, opool, ossd, p2d, w1, w2, w3, wpg, wpe, layer, nw, tm, emit_h):
    n, d = h2d.shape
    rows = lambda width: pl.BlockSpec((tm, width), lambda i: (i, 0))
    full = lambda a: pl.BlockSpec((None,) + a.shape[1:], lambda i: (layer, 0, 0),
                                  pipeline_mode=pl.Buffered(1))
    if emit_h:
        out_shape = (jax.ShapeDtypeStruct((n, d), F32), jax.ShapeDtypeStruct((n, d), BF16))
        out_specs = (rows(d), rows(d))
    else:
        out_shape = jax.ShapeDtypeStruct((n, d), F32)
        out_specs = rows(d)
    return pl.pallas_call(
        functools.partial(_out_kernel, emit_h=emit_h),
        grid=(n // tm,),
        in_specs=[rows(d), rows(D_HG), rows(D_POOL), rows(D_SSD_P),
                  pl.BlockSpec((None, tm, P_DIM), lambda i: (layer, i, 0)),
                  full(w1), full(w2), full(w3), full(wpg), full(wpe), pl.BlockSpec((1, d), lambda i: (0, 0))],
        out_specs=out_specs,
        out_shape=out_shape,
        compiler_params=pltpu.CompilerParams(
            dimension_semantics=("parallel",), vmem_limit_bytes=VMEM_LIMIT),
        name="out_proj",
    )(h2d, ohg, opool, ossd, p2d, w1, w2, w3, wpg, wpe, nw.reshape(1, d))


def _pad_groups(a):
    lead = a.shape[:-1]
    a = a.reshape(lead + (SSD_GROUPS, SSD_GW))
    a = jnp.pad(a, [(0, 0)] * len(lead) + [(0, 0), (0, SSD_GWP - SSD_GW)])
    return a.reshape(lead + (D_SSD_P,))


def _pad_lanes(a):
    return jnp.pad(a, [(0, 0)] * (a.ndim - 1) + [(0, LANES - a.shape[-1])])


C_POOL = 4 * D_HG
C_XBC = C_POOL + 2 * D_POOL
C_B = C_XBC + D_SSD
C_C = C_B + SSD_GROUPS * SSD_STATE
C_DT = C_C + SSD_GROUPS * SSD_STATE
C_Z = C_DT + SSD_HEADS
N_IN = C_Z + D_SSD
N_IN_MAIN = (N_IN // LANES) * LANES
WPREP_ROWS = 256


def _wprep_kernel(w_ref, tail_ref, hg_ref, pool_ref, ssd_ref):
    rows = w_ref.shape[1]
    lane = lax.broadcasted_iota(jnp.int32, (rows, LANES), 1)
    tail = jnp.where(lane < N_IN - N_IN_MAIN, tail_ref[0], 0.0)
    hg_ref[0] = w_ref[0, :, 0:C_POOL].astype(BF16)
    pool_ref[0] = w_ref[0, :, C_POOL:C_XBC].astype(BF16)

    def window(a0, width):
        if a0 + width <= N_IN_MAIN:
            return w_ref[0, :, a0:a0 + width]
        assert a0 + width == N_IN_MAIN + LANES
        return jnp.concatenate([w_ref[0, :, a0:N_IN_MAIN], tail], axis=1)

    lane_g = lax.broadcasted_iota(jnp.int32, (rows, SSD_GWP), 1)

    def group(a):
        a0 = (a // LANES) * LANES
        win = window(a0, SSD_GWP + LANES)
        if a != a0:
            win = pltpu.roll(win, SSD_GWP + LANES - (a - a0), 1)
        return jnp.where(lane_g < SSD_GW, win[:, :SSD_GWP], 0.0)

    for g in range(SSD_GROUPS):
        ssd_ref[0, :, g * SSD_GWP:(g + 1) * SSD_GWP] = group(C_XBC + g * SSD_GW).astype(BF16)
        zo = SSD_XBC_P + g * SSD_GWP
        ssd_ref[0, :, zo:zo + SSD_GWP] = group(C_Z + g * SSD_GW).astype(BF16)
    ssd_ref[0, :, D_SSD_P:SSD_XBC_P] = w_ref[0, :, C_B:C_DT].astype(BF16)
    dt = jnp.where(lane < SSD_HEADS, window(C_DT, LANES), 0.0)
    ssd_ref[0, :, SSD_XBC_P + D_SSD_P:] = dt.astype(BF16)


def _wprep_call(w_in):
    depth, d, n_in = w_in.shape
    assert n_in == N_IN and C_DT % LANES == 0 and d % WPREP_ROWS == 0
    out = lambda width: pl.BlockSpec((1, WPREP_ROWS, width), lambda i, r: (i, r, 0))
    return pl.pallas_call(
        _wprep_kernel,
        grid=(depth, d // WPREP_ROWS),
        in_specs=[pl.BlockSpec((1, WPREP_ROWS, N_IN_MAIN), lambda i, r: (i, r, 0)),
                  pl.BlockSpec((1, WPREP_ROWS, LANES), lambda i, r: (i, r, N_IN_MAIN // LANES))],
        out_specs=(out(C_POOL), out(C_XBC - C_POOL), out(SSD_COLS)),
        out_shape=(jax.ShapeDtypeStruct((depth, d, C_POOL), BF16),
                   jax.ShapeDtypeStruct((depth, d, C_XBC - C_POOL), BF16),
                   jax.ShapeDtypeStruct((depth, d, SSD_COLS), BF16)),
        compiler_params=pltpu.CompilerParams(
            dimension_semantics=("parallel", "parallel"), vmem_limit_bytes=VMEM_LIMIT),
        name="weight_prep",
    )(w_in, w_in)


def _wcast_kernel(wo_ref, wpg_ref, wpe_ref, o1_ref, o2_ref, o3_ref, opg_ref, ope_ref):
    o1_ref[0] = wo_ref[0, 0:D_HG, :].astype(BF16)
    o2_ref[0] = wo_ref[0, D_HG:D_HG + D_POOL, :].astype(BF16)
    base = D_HG + D_POOL
    for g in range(SSD_GROUPS):
        o3_ref[0, g * SSD_GWP:g * SSD_GWP + SSD_GW, :] = \
            wo_ref[0, base + g * SSD_GW:base + (g + 1) * SSD_GW, :].astype(BF16)
        o3_ref[0, g * SSD_GWP + SSD_GW:(g + 1) * SSD_GWP, :] = jnp.zeros((SSD_GWP - SSD_GW, wo_ref.shape[2]), BF16)
    opg_ref[0] = wpg_ref[0].astype(BF16)
    ope_ref[0] = wpe_ref[0].astype(BF16)


def _wcast_call(w_out, w_pg, w_pe):
    depth, d_mix, d = w_out.shape
    assert d_mix == D_HG + D_POOL + D_SSD
    layer = lambda a: pl.BlockSpec((1,) + a.shape[1:], lambda i: (i, 0, 0))
    shapes = [(depth, D_HG, d), (depth, D_POOL, d), (depth, D_SSD_P, d), w_pg.shape, w_pe.shape]
    outs = [jax.ShapeDtypeStruct(sh, BF16) for sh in shapes]
    return pl.pallas_call(
        _wcast_kernel,
        grid=(depth,),
        in_specs=[layer(w_out), layer(w_pg), layer(w_pe)],
        out_specs=tuple(layer(o) for o in outs),
        out_shape=tuple(outs),
        compiler_params=pltpu.CompilerParams(dimension_semantics=("parallel",), vmem_limit_bytes=VMEM_LIMIT),
        name="weight_cast",
    )(w_out, w_pg, w_pe)


def kernel(x, p, norm_w, w_in, hg_lb, hg_norm_w, pool_w, pool_scale, conv_w, conv_b, dt_bias, a_log, d_skip,
           ssd_norm_w, w_out, w_pe, w_pg, final_norm_w):
    b, s, d = x.shape
    depth = w_in.shape[0]
    n = b * s
    t_hg = min(2048, s)
    t_pool = min(1024, s)
    t_ssd = min(512, s)
    tm = min(1024, n)

    lb_all = jnp.cumsum(jax.nn.softmax(hg_lb.astype(F32), axis=0), axis=0)
    lb_all = lb_all - lb_all[0]


    w_hg_all, w_pool_all, w_ssd_all = _wprep_call(w_in)
    w_out_parts = _wcast_call(w_out, w_pg, w_pe)

    h = x.reshape(n, d)
    u = _rms_norm_call(h, norm_w[0], tm).reshape(b, s, d)
    out = None
    for i in range(depth):
        lb = lb_all[i].reshape(1, D_HG)
        o_hg = _hgrn_call(u, w_hg_all, i, jnp.log(lb) * LOG2E, jnp.log1p(-lb) * LOG2E,
                          hg_norm_w[i].reshape(1, D_HG), t_hg)
        o_pool = _pool_call(u, w_pool_all, i, pool_w[i].astype(BF16),
                            pool_scale[i].reshape(1, D_POOL), t_pool)
        xbc_split = lambda a: jnp.concatenate([_pad_groups(a[..., :D_SSD]), a[..., D_SSD:]], axis=-1)
        rep = lambda a: _pad_groups(jnp.repeat(a, SSD_HEAD_DIM)).reshape(1, D_SSD_P)
        o_ssd = _ssd_call(u, w_ssd_all, i, xbc_split(conv_w[i]), xbc_split(conv_b[i]).reshape(1, SSD_XBC_P),
                          _pad_lanes(dt_bias[i]).reshape(1, LANES), _pad_lanes(a_log[i]).reshape(1, LANES),
                          rep(d_skip[i]), _pad_groups(ssd_norm_w[i]).reshape(1, D_SSD_P), t_ssd)
        last = i == depth - 1
        nw_next = final_norm_w if last else norm_w[i + 1]
        res = _out_call(h, o_hg.reshape(n, D_HG), o_pool.reshape(n, D_POOL), o_ssd.reshape(n, D_SSD_P),
                        p.reshape(depth, n, P_DIM), *w_out_parts, i, nw_next, tm, emit_h=not last)
        if last:
            out = res
        else:
            h, u = res
            u = u.reshape(b, s, d)
    return out.reshape(b, s, d)
```
